```python
import math
import jax, jax.numpy as jnp
from jax import lax
import numpy as np

D_MODEL = 1024
BATCH = 2
SEQ = 8192
DEPTH = 2
DEC_BATCH = 128
DEC_SEQ = 8
PAST_LEN = 8192
PAGE_SIZE = 128

HEAD_DIM = 64
N_Q_HEADS = 12
N_KV_HEADS = 4
Q_PER_KV = N_Q_HEADS // N_KV_HEADS
WINDOW = 128
BLOCK = 128
N_MEM = 256
N_X_HEADS = 4
S5_WIDTH = 768
S5_GROUP = 16
S5_GROUPS = S5_WIDTH // S5_GROUP
S5_STATE = 64
D_FF = 3584
N_EXPERTS = 8
TOP_K = 2
EPS = 1e-5
N_A_LAYERS = (DEPTH + 1) // 2
N_B_LAYERS = DEPTH // 2
Q_A = N_Q_HEADS * HEAD_DIM
KV_A = N_KV_HEADS * HEAD_DIM
X_Q = N_X_HEADS * HEAD_DIM
MIX_A = Q_A + X_Q
MIX_B = S5_WIDTH + X_Q

kernel_name = 'hybrid_swa_sink_s5_memory_decoder_step'

f32 = jnp.float32


def rmsnorm(x, g):
    xf = x.astype(f32)
    y = xf * lax.rsqrt(jnp.mean(xf * xf, axis=-1, keepdims=True) + EPS)
    return (y * g.astype(f32)).astype(x.dtype)


def sink_attention(q, k, v, mask, sink):
    s = jnp.einsum('...qhgd,...khd->...hgqk', q, k).astype(f32) * (HEAD_DIM ** -0.5)
    s = jnp.where(mask, s, -jnp.inf)
    sk = sink.astype(f32)[:, :, None, None]
    m = jnp.maximum(jnp.max(s, axis=-1, keepdims=True), sk)
    p = jnp.exp(s - m)
    p = p / (jnp.sum(p, axis=-1, keepdims=True) + jnp.exp(sk - m))
    return jnp.einsum('...hgqk,...khd->...qhgd', p.astype(v.dtype), v)


def swa_prompt(q, k, v, sink):
    n, L = q.shape[0], q.shape[1]
    nb = L // BLOCK
    qb = q.reshape(n, nb, BLOCK, N_KV_HEADS, Q_PER_KV, HEAD_DIM)
    kb = k.reshape(n, nb, BLOCK, N_KV_HEADS, HEAD_DIM)
    vb = v.reshape(n, nb, BLOCK, N_KV_HEADS, HEAD_DIM)
    kk = jnp.concatenate([jnp.concatenate([jnp.zeros_like(kb[:, :1]), kb[:, :-1]], axis=1), kb], axis=2)
    vv = jnp.concatenate([jnp.concatenate([jnp.zeros_like(vb[:, :1]), vb[:, :-1]], axis=1), vb], axis=2)
    qi = jnp.arange(BLOCK)[:, None]
    kj = jnp.arange(2 * BLOCK)[None, :]
    dist = qi + BLOCK - kj
    kpos = jnp.arange(nb)[:, None, None] * BLOCK + kj - BLOCK
    mask = (dist >= 0) & (dist <= WINDOW) & (kpos >= 0)
    o = sink_attention(qb, kk, vv, mask[:, None, None], sink)
    w = min(WINDOW, L)
    return o.reshape(n, L, Q_A), k[:, L - w:], v[:, L - w:]


def swa_sample(q, k, v, k_buf, v_buf, sink):
    n, T = q.shape[0], q.shape[1]
    w = k_buf.shape[1]
    kk = jnp.concatenate([k_buf.astype(k.dtype), k], axis=1)
    vv = jnp.concatenate([v_buf.astype(v.dtype), v], axis=1)
    qg = q.reshape(n, T, N_KV_HEADS, Q_PER_KV, HEAD_DIM)
    dist = jnp.arange(T)[:, None] + w - jnp.arange(w + T)[None, :]
    mask = (dist >= 0) & (dist <= WINDOW)
    o = sink_attention(qg, kk, vv, mask, sink)
    return o.reshape(n, T, Q_A), kk[:, T:], vv[:, T:]


def memory_kv(mem, g, w):
    kv = rmsnorm(mem, g) @ w
    n = mem.shape[0]
    mk = kv[..., :X_Q].reshape(n, N_MEM, N_X_HEADS, HEAD_DIM)
    mv = kv[..., X_Q:].reshape(n, N_MEM, N_X_HEADS, HEAD_DIM)
    return mk, mv


def cross_attention(xq, mem_k, mem_v):
    n, L = xq.shape[0], xq.shape[1]
    q = xq.reshape(n, L, N_X_HEADS, HEAD_DIM)
    s = jnp.einsum('blhd,bmhd->bhlm', q, mem_k.astype(q.dtype)).astype(f32) * (HEAD_DIM ** -0.5)
    p = jax.nn.softmax(s, axis=-1).astype(q.dtype)
    o = jnp.einsum('bhlm,bmhd->blhd', p, mem_v.astype(q.dtype))
    return o.reshape(n, L, X_Q)


def _ssm_combine(e1, e2):
    a1r, a1i, b1r, b1i = e1
    a2r, a2i, b2r, b2i = e2
    return (a1r * a2r - a1i * a2i,
            a1r * a2i + a1i * a2r,
            a2r * b1r - a2i * b1i + b2r,
            a2r * b1i + a2i * b1r + b2i)


def s5_mixer(u, h0_re, h0_im, lam_re, lam_im, log_step, b_re, b_im, c_re, c_im, d_skip, w_glu, b_glu):
    n, L = u.shape[0], u.shape[1]
    uf = u.astype(f32).reshape(n, L, S5_GROUPS, S5_GROUP)
    step = jnp.exp(log_step.astype(f32))[:, None]
    lr, li = lam_re.astype(f32), lam_im.astype(f32)
    mag = jnp.exp(lr * step)
    ar, ai = mag * jnp.cos(li * step), mag * jnp.sin(li * step)
    den = lr * lr + li * li
    cr = ((ar - 1.0) * lr + ai * li) / den
    ci = (ai * lr - (ar - 1.0) * li) / den
    br, bi = b_re.astype(f32), b_im.astype(f32)
    bbr = cr[..., None] * br - ci[..., None] * bi
    bbi = cr[..., None] * bi + ci[..., None] * br
    xr = jnp.einsum('nlgh,gph->nlgp', uf, bbr)
    xi = jnp.einsum('nlgh,gph->nlgp', uf, bbi)
    h0r, h0i = h0_re.astype(f32), h0_im.astype(f32)
    xr = xr.at[:, 0].add(ar * h0r - ai * h0i)
    xi = xi.at[:, 0].add(ar * h0i + ai * h0r)
    a_r = jnp.broadcast_to(ar, xr.shape)
    a_i = jnp.broadcast_to(ai, xi.shape)
    _, _, hr, hi = lax.associative_scan(_ssm_combine, (a_r, a_i, xr, xi), axis=1)
    y = (jnp.einsum('ghp,nlgp->nlgh', c_re.astype(f32), hr)
         - jnp.einsum('ghp,nlgp->nlgh', c_im.astype(f32), hi)
         + d_skip.astype(f32).reshape(S5_GROUPS, S5_GROUP) * uf)
    z = jax.nn.gelu(y.reshape(n, L, S5_WIDTH))
    out = z * jax.nn.sigmoid(z @ w_glu.astype(f32) + b_glu.astype(f32))
    return out.astype(u.dtype), hr[:, -1], hi[:, -1]


def layer_a(x, mem_k, mem_v, g_mix, w_in, sink, w_out, k_buf=None, v_buf=None):
    n, L = x.shape[0], x.shape[1]
    p = rmsnorm(x, g_mix) @ w_in
    q = p[..., :Q_A]
    k = p[..., Q_A:Q_A + KV_A].reshape(n, L, N_KV_HEADS, HEAD_DIM)
    v = p[..., Q_A + KV_A:Q_A + 2 * KV_A].reshape(n, L, N_KV_HEADS, HEAD_DIM)
    xq = p[..., Q_A + 2 * KV_A:]
    sk = sink.reshape(N_KV_HEADS, Q_PER_KV)
    if k_buf is None:
        o, kc, vc = swa_prompt(q, k, v, sk)
    else:
        o, kc, vc = swa_sample(q, k, v, k_buf, v_buf, sk)
    xo = cross_attention(xq, mem_k, mem_v)
    return x + jnp.concatenate([o, xo], axis=-1) @ w_out, kc, vc


def layer_b(x, mem_k, mem_v, h0_re, h0_im, g_mix, w_in, lam_re, lam_im, log_step, b_re, b_im, c_re, c_im, d_skip, w_glu, b_glu, w_out):
    p = rmsnorm(x, g_mix) @ w_in
    o, hr, hi = s5_mixer(p[..., :S5_WIDTH], h0_re, h0_im, lam_re, lam_im, log_step, b_re, b_im, c_re, c_im, d_skip, w_glu, b_glu)
    xo = cross_attention(p[..., S5_WIDTH:], mem_k, mem_v)
    return x + jnp.concatenate([o, xo], axis=-1) @ w_out, hr, hi


def swiglu(h, w_gu, w_down):
    gu = h @ w_gu
    return (jax.nn.silu(gu[..., :D_FF]) * gu[..., D_FF:]) @ w_down


def moe(h, w_router, w_gu, w_down):
    logits = (h @ w_router).astype(f32)
    top_v, top_i = lax.top_k(logits, TOP_K)
    gates = jax.nn.softmax(top_v, axis=-1)
    comb = jnp.sum(jax.nn.one_hot(top_i, N_EXPERTS, dtype=f32) * gates[..., None], axis=-2)
    out = jnp.zeros_like(h)
    for e in range(N_EXPERTS):
        out = out + comb[..., e:e + 1].astype(h.dtype) * swiglu(h, w_gu[e], w_down[e])
    return out


def setup_inputs(seed: int = 0) -> dict:
    key = jax.random.key(seed)
    ks = iter(jax.random.split(key, 48))

    def nrm(shape, scale):
        return jax.random.normal(next(ks), shape, f32) * scale

    cache_w = min(WINDOW, PAST_LEN)
    d_in = D_MODEL ** -0.5
    lam_im0 = math.pi * jnp.arange(S5_STATE, dtype=f32)
    return {
        'x_prompt': nrm((BATCH, SEQ, D_MODEL), 1.0),
        'x_sample': nrm((DEC_BATCH, DEC_SEQ, D_MODEL), 1.0),
        'mem_prompt': nrm((BATCH, N_MEM, D_MODEL), 1.0),
        'cache_swa_k': nrm((N_A_LAYERS, DEC_BATCH, cache_w, N_KV_HEADS, HEAD_DIM), 1.0),
        'cache_swa_v': nrm((N_A_LAYERS, DEC_BATCH, cache_w, N_KV_HEADS, HEAD_DIM), 1.0),
        'state_ssm_re': nrm((N_B_LAYERS, DEC_BATCH, S5_GROUPS, S5_STATE), 0.3),
        'state_ssm_im': nrm((N_B_LAYERS, DEC_BATCH, S5_GROUPS, S5_STATE), 0.3),
        'cache_mem_k': nrm((DEPTH, DEC_BATCH, N_MEM, N_X_HEADS, HEAD_DIM), 1.0),
        'cache_mem_v': nrm((DEPTH, DEC_BATCH, N_MEM, N_X_HEADS, HEAD_DIM), 1.0),
        'g_mix': 1.0 + nrm((DEPTH, D_MODEL), 0.01),
        'g_ffn': 1.0 + nrm((DEPTH, D_MODEL), 0.01),
        'g_mem': 1.0 + nrm((DEPTH, D_MODEL), 0.01),
        'g_final': 1.0 + nrm((D_MODEL,), 0.01),
        'w_mem_kv': nrm((DEPTH, D_MODEL, 2 * X_Q), d_in),
        'w_in_a': nrm((N_A_LAYERS, D_MODEL, Q_A + 2 * KV_A + X_Q), d_in),
        'sinks': nrm((N_A_LAYERS, N_Q_HEADS), 0.5),
        'w_out_a': nrm((N_A_LAYERS, MIX_A, D_MODEL), MIX_A ** -0.5),
        'w_in_b': nrm((N_B_LAYERS, D_MODEL, MIX_B), d_in),
        'lam_re': -0.5 + nrm((N_B_LAYERS, S5_GROUPS, S5_STATE), 0.01),
        'lam_im': lam_im0 + nrm((N_B_LAYERS, S5_GROUPS, S5_STATE), 0.01),
        'log_step': jax.random.uniform(next(ks), (N_B_LAYERS, S5_GROUPS), f32, math.log(1e-3), math.log(1e-1)),
        'b_re': nrm((N_B_LAYERS, S5_GROUPS, S5_STATE, S5_GROUP), (2 * S5_GROUP) ** -0.5),
        'b_im': nrm((N_B_LAYERS, S5_GROUPS, S5_STATE, S5_GROUP), (2 * S5_GROUP) ** -0.5),
        'c_re': nrm((N_B_LAYERS, S5_GROUPS, S5_GROUP, S5_STATE), (2 * S5_STATE) ** -0.5),
        'c_im': nrm((N_B_LAYERS, S5_GROUPS, S5_GROUP, S5_STATE), (2 * S5_STATE) ** -0.5),
        'd_skip': nrm((N_B_LAYERS, S5_WIDTH), 1.0),
        'w_glu': nrm((N_B_LAYERS, S5_WIDTH, S5_WIDTH), S5_WIDTH ** -0.5),
        'b_glu': nrm((N_B_LAYERS, S5_WIDTH), 0.01),
        'w_out_b': nrm((N_B_LAYERS, MIX_B, D_MODEL), MIX_B ** -0.5),
        'w_ffn_gu': nrm((N_A_LAYERS, D_MODEL, 2 * D_FF), d_in),
        'w_ffn_down': nrm((N_A_LAYERS, D_FF, D_MODEL), D_FF ** -0.5),
        'w_router': nrm((N_B_LAYERS, D_MODEL, N_EXPERTS), d_in),
        'w_exp_gu': nrm((N_B_LAYERS, N_EXPERTS, D_MODEL, 2 * D_FF), d_in),
        'w_exp_down': nrm((N_B_LAYERS, N_EXPERTS, D_FF, D_MODEL), D_FF ** -0.5),
    }


def reference(x_prompt, x_sample, mem_prompt, cache_swa_k, cache_swa_v, state_ssm_re, state_ssm_im,
              cache_mem_k, cache_mem_v, g_mix, g_ffn, g_mem, g_final, w_mem_kv, w_in_a, sinks, w_out_a,
              w_in_b, lam_re, lam_im, log_step, b_re, b_im, c_re, c_im, d_skip, w_glu, b_glu, w_out_b,
              w_ffn_gu, w_ffn_down, w_router, w_exp_gu, w_exp_down):
    yp, ys = x_prompt, x_sample
    swa_kp, swa_vp, swa_ks, swa_vs = [], [], [], []
    ssm_rp, ssm_ip, ssm_rs, ssm_is = [], [], [], []
    mem_kp, mem_vp = [], []
    for i in range(DEPTH):
        li = i // 2
        mk_p, mv_p = memory_kv(mem_prompt, g_mem[i], w_mem_kv[i])
        mem_kp.append(mk_p)
        mem_vp.append(mv_p)
        if i % 2 == 0:
            yp, kc, vc = layer_a(yp, mk_p, mv_p, g_mix[i], w_in_a[li], sinks[li], w_out_a[li])
            swa_kp.append(kc)
            swa_vp.append(vc)
            ys, kc, vc = layer_a(ys, cache_mem_k[i], cache_mem_v[i], g_mix[i], w_in_a[li], sinks[li], w_out_a[li],
                                 cache_swa_k[li], cache_swa_v[li])
            swa_ks.append(kc)
            swa_vs.append(vc)
            yp = yp + swiglu(rmsnorm(yp, g_ffn[i]), w_ffn_gu[li], w_ffn_down[li])
            ys = ys + swiglu(rmsnorm(ys, g_ffn[i]), w_ffn_gu[li], w_ffn_down[li])
        else:
            ssm_w = (lam_re[li], lam_im[li], log_step[li], b_re[li], b_im[li], c_re[li], c_im[li],
                     d_skip[li], w_glu[li], b_glu[li])
            h0 = jnp.zeros((yp.shape[0], S5_GROUPS, S5_STATE), f32)
            yp, hr, hi = layer_b(yp, mk_p, mv_p, h0, h0, g_mix[i], w_in_b[li], *ssm_w, w_out_b[li])
            ssm_rp.append(hr)
            ssm_ip.append(hi)
            ys, hr, hi = layer_b(ys, cache_mem_k[i], cache_mem_v[i], state_ssm_re[li], state_ssm_im[li],
                                 g_mix[i], w_in_b[li], *ssm_w, w_out_b[li])
            ssm_rs.append(hr)
            ssm_is.append(hi)
            yp = yp + moe(rmsnorm(yp, g_ffn[i]), w_router[li], w_exp_gu[li], w_exp_down[li])
            ys = ys + moe(rmsnorm(ys, g_ffn[i]), w_router[li], w_exp_gu[li], w_exp_down[li])
    y_prompt = rmsnorm(yp, g_final)
    y_sample = rmsnorm(ys, g_final)
    return (y_prompt, y_sample,
            jnp.stack(swa_kp), jnp.stack(swa_vp), jnp.stack(ssm_rp), jnp.stack(ssm_ip),
            jnp.stack(mem_kp), jnp.stack(mem_vp),
            jnp.stack(swa_ks), jnp.stack(swa_vs), jnp.stack(ssm_rs), jnp.stack(ssm_is))
```

```python
import functools

import jax
import jax.numpy as jnp
from jax import lax
from jax.experimental import pallas as pl
from jax.experimental.pallas import tpu as pltpu

F32 = jnp.float32
BF16 = jnp.bfloat16
I32 = jnp.int32

HEAD_DIM = 64
WINDOW = 128
BLOCK = 128
EPS = 1e-5
SCALE = HEAD_DIM ** -0.5
S5_GROUP = 16
SUBSEQ = 8
SUBLANES = 8
LANE_WIN = 512
VMEM_LIMIT = 56 * 1024 * 1024

_NT = (((1,), (1,)), ((), ()))


def _params(sem, vmem=VMEM_LIMIT):
    return pltpu.CompilerParams(dimension_semantics=sem, vmem_limit_bytes=vmem)


def _rms(x, g):
    return x * lax.rsqrt(jnp.mean(x * x, axis=-1, keepdims=True) + EPS) * g


def _softmax(s):
    m = jnp.max(s, axis=-1, keepdims=True)
    p = jnp.exp(s - m)
    return p / jnp.sum(p, axis=-1, keepdims=True)


def _proj_kernel(x_ref, g_ref, w_ref, o_ref):
    h = _rms(x_ref[...], g_ref[...]).astype(BF16)
    o_ref[...] = jnp.dot(h, w_ref[...], preferred_element_type=F32)


def _proj(x, g, w, tm):
    rows, d = x.shape
    n = w.shape[1]
    return pl.pallas_call(
        _proj_kernel,
        grid=(rows // tm,),
        in_specs=[pl.BlockSpec((tm, d), lambda i: (i, 0)),
                  pl.BlockSpec((1, d), lambda i: (0, 0)),
                  pl.BlockSpec((d, n), lambda i: (0, 0))],
        out_specs=pl.BlockSpec((tm, n), lambda i: (i, 0)),
        out_shape=jax.ShapeDtypeStruct((rows, n), F32),
        compiler_params=_params(("parallel",)),
        name="rms_proj",
    )(x, g.reshape(1, d), w.astype(BF16))


def _sink_attention_group(q_rows, k_h, v_h, mask, sink_col):
    s = lax.dot_general(q_rows, k_h, _NT, preferred_element_type=F32)
    s = jnp.where(mask, s, -jnp.inf)
    m = jnp.maximum(jnp.max(s, axis=-1, keepdims=True), sink_col)
    p = jnp.exp(s - m)
    den = jnp.sum(p, axis=-1, keepdims=True) + jnp.exp(sink_col - m)
    return jnp.dot((p / den).astype(BF16), v_h, preferred_element_type=F32)


def _cross_attention(xq, mk, mv, n_heads, mix_ref, rows, col0):
    xq = (xq * SCALE).astype(BF16)
    for h in range(n_heads):
        sl = slice(h * HEAD_DIM, (h + 1) * HEAD_DIM)
        s = lax.dot_general(xq[:, sl], mk[:, sl], _NT, preferred_element_type=F32)
        o = jnp.dot(_softmax(s).astype(BF16), mv[:, sl], preferred_element_type=F32)
        mix_ref[rows, col0 + h * HEAD_DIM:col0 + (h + 1) * HEAD_DIM] = o


def _sink_column(sink_ref, head0, group, rows_per_head):
    row = lax.broadcasted_iota(I32, (group * rows_per_head, 1), 0)
    col = jnp.full((group * rows_per_head, 1), sink_ref[0, head0 + group - 1], F32)
    for g in range(group - 2, -1, -1):
        col = jnp.where(row < (g + 1) * rows_per_head, sink_ref[0, head0 + g], col)
    return col


def _out_proj(x_ref, mix_ref, wo_ref, o_ref):
    o_ref[...] = x_ref[...] + jnp.dot(mix_ref[...].astype(BF16), wo_ref[...],
                                      preferred_element_type=F32)


def _mixa_prompt_kernel(sink_ref, q_ref, kp_ref, kc_ref, vp_ref, vc_ref, xq_ref, x_ref,
                        mk_ref, mv_ref, wo_ref, o_ref, mix_ref, *, n_kv, group, n_x):
    i = pl.program_id(1)
    q = q_ref[...] * SCALE
    kk = jnp.concatenate([kp_ref[...], kc_ref[...]], axis=0).astype(BF16)
    vv = jnp.concatenate([vp_ref[...], vc_ref[...]], axis=0).astype(BF16)
    m_rows = group * BLOCK
    qi = lax.rem(lax.broadcasted_iota(I32, (m_rows, 2 * BLOCK), 0), BLOCK)
    kj = lax.broadcasted_iota(I32, (m_rows, 2 * BLOCK), 1)
    dist = qi + BLOCK - kj
    first = jnp.where(i > 0, 0, BLOCK)
    mask = (dist >= 0) & (dist <= WINDOW) & (kj >= first)
    for h in range(n_kv):
        sl = slice(h * HEAD_DIM, (h + 1) * HEAD_DIM)
        qh = jnp.concatenate(
            [q[:, (h * group + g) * HEAD_DIM:(h * group + g + 1) * HEAD_DIM] for g in range(group)],
            axis=0).astype(BF16)
        o = _sink_attention_group(qh, kk[:, sl], vv[:, sl], mask,
                                  _sink_column(sink_ref, h * group, group, BLOCK))
        for g in range(group):
            c0 = (h * group + g) * HEAD_DIM
            mix_ref[:, c0:c0 + HEAD_DIM] = o[g * BLOCK:(g + 1) * BLOCK]
    q_a = n_kv * group * HEAD_DIM
    _cross_attention(xq_ref[...], mk_ref[...].astype(BF16), mv_ref[...].astype(BF16), n_x,
                     mix_ref, slice(None), q_a)
    _out_proj(x_ref, mix_ref, wo_ref, o_ref)


def _mixa_prompt(p, x, memkv, sinks, w_out, n_batch, n_kv, group, n_x, n_mem):
    rows, d = x.shape
    nb = rows // n_batch // BLOCK
    q_a = n_kv * group * HEAD_DIM
    kv_a = n_kv * HEAD_DIM
    x_q = n_x * HEAD_DIM
    kcol = q_a // kv_a
    vcol = (q_a + kv_a) // kv_a
    xcol = (q_a + 2 * kv_a) // x_q
    cur = lambda b, i: b * nb + i
    prev = lambda b, i: b * nb + jnp.maximum(i - 1, 0)
    kern = functools.partial(_mixa_prompt_kernel, n_kv=n_kv, group=group, n_x=n_x)
    return pl.pallas_call(
        kern,
        grid=(n_batch, nb),
        in_specs=[
            pl.BlockSpec(memory_space=pltpu.SMEM),
            pl.BlockSpec((BLOCK, q_a), lambda b, i: (cur(b, i), 0)),
            pl.BlockSpec((BLOCK, kv_a), lambda b, i: (prev(b, i), kcol)),
            pl.BlockSpec((BLOCK, kv_a), lambda b, i: (cur(b, i), kcol)),
            pl.BlockSpec((BLOCK, kv_a), lambda b, i: (prev(b, i), vcol)),
            pl.BlockSpec((BLOCK, kv_a), lambda b, i: (cur(b, i), vcol)),
            pl.BlockSpec((BLOCK, x_q), lambda b, i: (cur(b, i), xcol)),
            pl.BlockSpec((BLOCK, d), lambda b, i: (cur(b, i), 0)),
            pl.BlockSpec((n_mem, x_q), lambda b, i: (b, 0)),
            pl.BlockSpec((n_mem, x_q), lambda b, i: (b, 1)),
            pl.BlockSpec((q_a + x_q, d), lambda b, i: (0, 0)),
        ],
        out_specs=pl.BlockSpec((BLOCK, d), lambda b, i: (cur(b, i), 0)),
        out_shape=jax.ShapeDtypeStruct((rows, d), F32),
        scratch_shapes=[pltpu.VMEM((BLOCK, q_a + x_q), F32)],
        compiler_params=_params(("parallel", "parallel")),
        name="mix_a_prompt",
    )(sinks.reshape(1, -1), p, p, p, p, p, p, x, memkv, memkv, w_out.astype(BF16))


def _mixa_sample_kernel(sink_ref, p_ref, x_ref, ck_ref, cv_ref, mk_ref, mv_ref, wo_ref,
                        o_ref, ok_ref, ov_ref, mix_ref, *, n_kv, group, n_x, t_new, bt):
    w = ck_ref.shape[1]
    q_a = n_kv * group * HEAD_DIM
    kv_a = n_kv * HEAD_DIM
    m_rows = group * t_new
    ti = lax.rem(lax.broadcasted_iota(I32, (m_rows, w + t_new), 0), t_new)
    kj = lax.broadcasted_iota(I32, (m_rows, w + t_new), 1)
    dist = ti + w - kj
    mask = (dist >= 0) & (dist <= WINDOW)

    def body(bb, carry):
        rows = pl.ds(pl.multiple_of(bb * t_new, t_new), t_new)
        q = p_ref[rows, 0:q_a] * SCALE
        k_new = p_ref[rows, q_a:q_a + kv_a]
        v_new = p_ref[rows, q_a + kv_a:q_a + 2 * kv_a]
        xq = p_ref[rows, q_a + 2 * kv_a:]
        k_old = ck_ref[bb]
        v_old = cv_ref[bb]
        ok_ref[bb, 0:w - t_new, :] = k_old[t_new:, :]
        ok_ref[bb, w - t_new:w, :] = k_new
        ov_ref[bb, 0:w - t_new, :] = v_old[t_new:, :]
        ov_ref[bb, w - t_new:w, :] = v_new
        kk = jnp.concatenate([k_old, k_new], axis=0).astype(BF16)
        vv = jnp.concatenate([v_old, v_new], axis=0).astype(BF16)
        for h in range(n_kv):
            sl = slice(h * HEAD_DIM, (h + 1) * HEAD_DIM)
            qh = jnp.concatenate(
                [q[:, (h * group + g) * HEAD_DIM:(h * group + g + 1) * HEAD_DIM]
                 for g in range(group)], axis=0).astype(BF16)
            o = _sink_attention_group(qh, kk[:, sl], vv[:, sl], mask,
                                      _sink_column(sink_ref, h * group, group, t_new))
            for g in range(group):
                c0 = (h * group + g) * HEAD_DIM
                mix_ref[rows, c0:c0 + HEAD_DIM] = o[g * t_new:(g + 1) * t_new]
        _cross_attention(xq, mk_ref[bb].astype(BF16), mv_ref[bb].astype(BF16), n_x,
                         mix_ref, rows, q_a)
        return carry

    lax.fori_loop(0, bt, body, 0)
    _out_proj(x_ref, mix_ref, wo_ref, o_ref)


def _mixa_sample(p, x, cache_k, cache_v, mem_k, mem_v, sinks, w_out, n_kv, group, n_x, t_new, bt):
    rows, d = x.shape
    n_dec, w, kv_a = cache_k.shape
    n_mem, x_q = mem_k.shape[1:]
    q_a = n_kv * group * HEAD_DIM
    tm = bt * t_new
    kern = functools.partial(_mixa_sample_kernel, n_kv=n_kv, group=group, n_x=n_x,
                             t_new=t_new, bt=bt)
    return pl.pallas_call(
        kern,
        grid=(n_dec // bt,),
        in_specs=[
            pl.BlockSpec(memory_space=pltpu.SMEM),
            pl.BlockSpec((tm, p.shape[1]), lambda i: (i, 0)),
            pl.BlockSpec((tm, d), lambda i: (i, 0)),
            pl.BlockSpec((bt, w, kv_a), lambda i: (i, 0, 0)),
            pl.BlockSpec((bt, w, kv_a), lambda i: (i, 0, 0)),
            pl.BlockSpec((bt, n_mem, x_q), lambda i: (i, 0, 0)),
            pl.BlockSpec((bt, n_mem, x_q), lambda i: (i, 0, 0)),
            pl.BlockSpec((q_a + x_q, d), lambda i: (0, 0)),
        ],
        out_specs=[pl.BlockSpec((tm, d), lambda i: (i, 0)),
                   pl.BlockSpec((bt, w, kv_a), lambda i: (i, 0, 0)),
                   pl.BlockSpec((bt, w, kv_a), lambda i: (i, 0, 0))],
        out_shape=[jax.ShapeDtypeStruct((rows, d), F32),
                   jax.ShapeDtypeStruct((n_dec, w, kv_a), F32),
                   jax.ShapeDtypeStruct((n_dec, w, kv_a), F32)],
        scratch_shapes=[pltpu.VMEM((tm, q_a + x_q), F32)],
        compiler_params=_params(("parallel",)),
        name="mix_a_sample",
    )(sinks.reshape(1, -1), p, x, cache_k, cache_v, mem_k, mem_v, w_out.astype(BF16))


def _tailb_prompt_kernel(s_ref, xq_ref, x_ref, mk_ref, mv_ref, wo_ref, o_ref, mix_ref, *, n_x):
    width = s_ref.shape[1]
    mix_ref[:, 0:width] = s_ref[...].astype(F32)
    _cross_attention(xq_ref[...], mk_ref[...].astype(BF16), mv_ref[...].astype(BF16), n_x,
                     mix_ref, slice(None), width)
    _out_proj(x_ref, mix_ref, wo_ref, o_ref)


def _tailb_prompt(s_out, p, x, memkv, w_out, n_batch, n_x, n_mem, tm):
    rows, d = x.shape
    width = s_out.shape[1]
    x_q = n_x * HEAD_DIM
    per_b = rows // n_batch // tm
    kern = functools.partial(_tailb_prompt_kernel, n_x=n_x)
    return pl.pallas_call(
        kern,
        grid=(rows // tm,),
        in_specs=[
            pl.BlockSpec((tm, width), lambda i: (i, 0)),
            pl.BlockSpec((tm, x_q), lambda i: (i, width // x_q)),
            pl.BlockSpec((tm, d), lambda i: (i, 0)),
            pl.BlockSpec((n_mem, x_q), lambda i: (i // per_b, 0)),
            pl.BlockSpec((n_mem, x_q), lambda i: (i // per_b, 1)),
            pl.BlockSpec((width + x_q, d), lambda i: (0, 0)),
        ],
        out_specs=pl.BlockSpec((tm, d), lambda i: (i, 0)),
        out_shape=jax.ShapeDtypeStruct((rows, d), F32),
        scratch_shapes=[pltpu.VMEM((tm, width + x_q), F32)],
        compiler_params=_params(("parallel",)),
        name="tail_b_prompt",
    )(s_out, p, x, memkv, memkv, w_out.astype(BF16))


def _tailb_sample_kernel(s_ref, xq_ref, x_ref, mk_ref, mv_ref, wo_ref, o_ref, mix_ref,
                         *, n_x, t_new, bt):
    width = s_ref.shape[1]
    mix_ref[:, 0:width] = s_ref[...].astype(F32)

    def body(bb, carry):
        rows = pl.ds(pl.multiple_of(bb * t_new, t_new), t_new)
        _cross_attention(xq_ref[rows, :], mk_ref[bb].astype(BF16), mv_ref[bb].astype(BF16), n_x,
                         mix_ref, rows, width)
        return carry

    lax.fori_loop(0, bt, body, 0)
    _out_proj(x_ref, mix_ref, wo_ref, o_ref)


def _tailb_sample(s_out, p, x, mem_k, mem_v, w_out, n_x, t_new, bt):
    rows, d = x.shape
    width = s_out.shape[1]
    n_dec, n_mem, x_q = mem_k.shape
    tm = bt * t_new
    kern = functools.partial(_tailb_sample_kernel, n_x=n_x, t_new=t_new, bt=bt)
    return pl.pallas_call(
        kern,
        grid=(n_dec // bt,),
        in_specs=[
            pl.BlockSpec((tm, width), lambda i: (i, 0)),
            pl.BlockSpec((tm, x_q), lambda i: (i, width // x_q)),
            pl.BlockSpec((tm, d), lambda i: (i, 0)),
            pl.BlockSpec((bt, n_mem, x_q), lambda i: (i, 0, 0)),
            pl.BlockSpec((bt, n_mem, x_q), lambda i: (i, 0, 0)),
            pl.BlockSpec((width + x_q, d), lambda i: (0, 0)),
        ],
        out_specs=pl.BlockSpec((tm, d), lambda i: (i, 0)),
        out_shape=jax.ShapeDtypeStruct((rows, d), F32),
        scratch_shapes=[pltpu.VMEM((tm, width + x_q), F32)],
        compiler_params=_params(("parallel",)),
        name="tail_b_sample",
    )(s_out, p, x, mem_k, mem_v, w_out.astype(BF16))


def _s5_disc_kernel(lr_ref, li_ref, ls_ref, br_ref, bi_ref,
                    ar_ref, ai_ref, apr_ref, api_ref, bbr_ref, bbi_ref, *, n_square):
    lr, li = lr_ref[...], li_ref[...]
    step = jnp.exp(ls_ref[...])
    mag = jnp.exp(lr * step)
    ar, ai = mag * jnp.cos(li * step), mag * jnp.sin(li * step)
    den = lr * lr + li * li
    cr = ((ar - 1.0) * lr + ai * li) / den
    ci = (ai * lr - (ar - 1.0) * li) / den
    br, bi = br_ref[...], bi_ref[...]
    bbr_ref[...] = cr * br - ci * bi
    bbi_ref[...] = cr * bi + ci * br
    ar_ref[...] = ar
    ai_ref[...] = ai
    pr, pi = ar, ai
    for _ in range(n_square):
        pr, pi = pr * pr - pi * pi, 2.0 * pr * pi
    apr_ref[...] = pr
    api_ref[...] = pi


def _s5_weights(lam_re, lam_im, log_step, b_re, b_im, c_re, c_im, sub_len):
    g, p = lam_re.shape
    hh = b_re.shape[2]
    rep = lambda a: jnp.repeat(a, hh, axis=0)
    flat = lambda a: a.transpose(0, 2, 1).reshape(g * hh, p)
    n_square = sub_len.bit_length() - 1
    assert 1 << n_square == sub_len
    shp = jax.ShapeDtypeStruct((g * hh, p), F32)
    ar, ai, apr, api, bbr, bbi = pl.pallas_call(
        functools.partial(_s5_disc_kernel, n_square=n_square),
        out_shape=[shp] * 6,
        name="s5_discretise",
    )(rep(lam_re), rep(lam_im), rep(jnp.broadcast_to(log_step[:, None], (g, p))),
      flat(b_re), flat(b_im))
    row = lambda a: a[::hh].reshape(1, g * p)
    gb = 256 // hh
    nblk = g // gb
    eye = jnp.eye(gb, dtype=F32)
    bd_in = lambda a: (a.reshape(nblk, gb, hh, 1, p) * eye[None, :, None, :, None]
                       ).reshape(nblk, gb * hh, gb * p)
    wb = jnp.concatenate([bd_in(bbr), bd_in(bbi)], axis=-1).astype(BF16)
    bd_out = lambda c: (c.reshape(nblk, gb, hh, p).transpose(0, 1, 3, 2)[:, :, :, None, :]
                        * eye[None, :, None, :, None]).reshape(nblk, gb * p, gb * hh)
    wc = jnp.concatenate([bd_out(c_re), -bd_out(c_im)], axis=1).astype(BF16)
    return row(ar), row(ai), row(apr), row(api), wb, wc


def _cstep(ar, ai, hr, hi, xr, xi):
    return ar * hr - ai * hi + xr, ar * hi + ai * hr + xi


def _glu_out(y, u, d_ref, wg_ref, bg_ref):
    z = jax.nn.gelu(y + d_ref[...] * u)
    gate = jnp.dot(z.astype(BF16), wg_ref[...], preferred_element_type=F32) + bg_ref[...]
    return z * jax.nn.sigmoid(gate)


def _s5_prompt_kernel(u_ref, ar_ref, ai_ref, apr_ref, api_ref, wb_ref, wc_ref, d_ref, wg_ref,
                      bg_ref, o_ref, hf_ref, xs_ref, y_ref, carry_ref, *, n_steps):
    c = pl.program_id(1)
    nblk, ublk, two_cw = wb_ref.shape
    cw = two_cw // 2
    n_state = nblk * cw

    @pl.when(c == 0)
    def _():
        carry_ref[...] = jnp.zeros_like(carry_ref)

    row8 = lax.broadcasted_iota(I32, (SUBLANES, LANE_WIN), 0)
    for j in range(nblk):
        xs_ref[...] = jnp.dot(u_ref[:, j * ublk:(j + 1) * ublk], wb_ref[j],
                              preferred_element_type=F32)
        for lb in range(cw // LANE_WIN):
            re = slice(lb * LANE_WIN, (lb + 1) * LANE_WIN)
            im = slice(cw + lb * LANE_WIN, cw + (lb + 1) * LANE_WIN)
            gre = slice(j * cw + lb * LANE_WIN, j * cw + (lb + 1) * LANE_WIN)
            gim = slice(n_state + j * cw + lb * LANE_WIN, n_state + j * cw + (lb + 1) * LANE_WIN)
            bc = lambda r: jnp.broadcast_to(r[:, gre], (SUBLANES, LANE_WIN))
            ar, ai, apr, api = bc(ar_ref), bc(ai_ref), bc(apr_ref), bc(api_ref)
            c_r = jnp.broadcast_to(carry_ref[:, gre], (SUBLANES, LANE_WIN))
            c_i = jnp.broadcast_to(carry_ref[:, gim], (SUBLANES, LANE_WIN))

            def rows(i):
                return pl.ds(pl.multiple_of(i * SUBLANES, SUBLANES), SUBLANES)

            def pass1(i, h):
                return _cstep(ar, ai, h[0], h[1], xs_ref[rows(i), re], xs_ref[rows(i), im])

            zero = jnp.zeros((SUBLANES, LANE_WIN), F32)
            e_r, e_i = lax.fori_loop(0, n_steps, pass1, (zero, zero), unroll=4)
            s_r = jnp.where(row8 == 0, c_r, 0.0)
            s_i = jnp.where(row8 == 0, c_i, 0.0)
            for _ in range(SUBSEQ - 1):
                n_r, n_i = _cstep(apr, api, s_r, s_i, e_r, e_i)
                s_r = jnp.where(row8 == 0, c_r, pltpu.roll(n_r, 1, 0))
                s_i = jnp.where(row8 == 0, c_i, pltpu.roll(n_i, 1, 0))
            n_r, n_i = _cstep(apr, api, s_r, s_i, e_r, e_i)
            carry_ref[:, gre] = n_r[SUBSEQ - 1:SUBSEQ, :]
            carry_ref[:, gim] = n_i[SUBSEQ - 1:SUBSEQ, :]

            def pass2(i, h):
                h = _cstep(ar, ai, h[0], h[1], xs_ref[rows(i), re], xs_ref[rows(i), im])
                xs_ref[rows(i), re] = h[0]
                xs_ref[rows(i), im] = h[1]
                return h

            lax.fori_loop(0, n_steps, pass2, (s_r, s_i), unroll=4)
        y_ref[:, j * ublk:(j + 1) * ublk] = jnp.dot(xs_ref[...].astype(BF16), wc_ref[j],
                                                    preferred_element_type=F32)
    o_ref[...] = _glu_out(y_ref[...], u_ref[...].astype(F32), d_ref, wg_ref, bg_ref).astype(o_ref.dtype)
    hf_ref[0] = carry_ref[...]


def _s5_prompt(u_perm, sw, d_skip, w_glu, b_glu, n_batch, chunk):
    rows, width = u_perm.shape
    ar, ai, apr, api, wb, wc = sw
    nblk, ublk, two_cw = wb.shape
    n_state = ar.shape[1]
    nchunk = rows // n_batch // chunk
    const2 = lambda b, c: (0, 0)
    const3 = lambda b, c: (0, 0, 0)
    kern = functools.partial(_s5_prompt_kernel, n_steps=chunk // SUBSEQ)
    return pl.pallas_call(
        kern,
        grid=(n_batch, nchunk),
        in_specs=[
            pl.BlockSpec((chunk, width), lambda b, c: (b * nchunk + c, 0)),
            pl.BlockSpec((1, n_state), const2), pl.BlockSpec((1, n_state), const2),
            pl.BlockSpec((1, n_state), const2), pl.BlockSpec((1, n_state), const2),
            pl.BlockSpec(wb.shape, const3), pl.BlockSpec(wc.shape, const3),
            pl.BlockSpec((1, width), const2), pl.BlockSpec((width, width), const2),
            pl.BlockSpec((1, width), const2),
        ],
        out_specs=[pl.BlockSpec((chunk, width), lambda b, c: (b * nchunk + c, 0)),
                   pl.BlockSpec((1, 1, 2 * n_state), lambda b, c: (b, 0, 0))],
        out_shape=[jax.ShapeDtypeStruct((rows, width), BF16),
                   jax.ShapeDtypeStruct((n_batch, 1, 2 * n_state), F32)],
        scratch_shapes=[pltpu.VMEM((chunk, two_cw), F32), pltpu.VMEM((chunk, width), F32),
                        pltpu.VMEM((1, 2 * n_state), F32)],
        compiler_params=_params(("arbitrary", "arbitrary")),
        name="s5_prompt",
    )(u_perm, ar, ai, apr, api, wb, wc, d_skip.reshape(1, -1), w_glu.astype(BF16),
      b_glu.reshape(1, -1))


def _s5_sample_kernel(u_ref, h0r_ref, h0i_ref, ar_ref, ai_ref, wb_ref, wc_ref, d_ref, wg_ref,
                      bg_ref, o_ref, hr_ref, hi_ref, xs_ref, y_ref, *, t_new):
    nblk, ublk, two_cw = wb_ref.shape
    cw = two_cw // 2
    n_seq = h0r_ref.shape[0]
    for j in range(nblk):
        xs_ref[...] = jnp.dot(u_ref[:, j * ublk:(j + 1) * ublk], wb_ref[j],
                              preferred_element_type=F32)
        for lb in range(cw // LANE_WIN):
            re = slice(lb * LANE_WIN, (lb + 1) * LANE_WIN)
            im = slice(cw + lb * LANE_WIN, cw + (lb + 1) * LANE_WIN)
            gre = slice(j * cw + lb * LANE_WIN, j * cw + (lb + 1) * LANE_WIN)
            ar = jnp.broadcast_to(ar_ref[:, gre], (SUBLANES, LANE_WIN))
            ai = jnp.broadcast_to(ai_ref[:, gre], (SUBLANES, LANE_WIN))

            def group_body(rg, carry):
                base = pl.multiple_of(rg * SUBLANES, SUBLANES)
                h = (h0r_ref[pl.ds(base, SUBLANES), gre], h0i_ref[pl.ds(base, SUBLANES), gre])
                for t in range(t_new):
                    r = pl.ds(t * n_seq + base, SUBLANES)
                    h = _cstep(ar, ai, h[0], h[1], xs_ref[r, re], xs_ref[r, im])
                    xs_ref[r, re] = h[0]
                    xs_ref[r, im] = h[1]
                hr_ref[pl.ds(base, SUBLANES), gre] = h[0]
                hi_ref[pl.ds(base, SUBLANES), gre] = h[1]
                return carry

            lax.fori_loop(0, n_seq // SUBLANES, group_body, 0)
        y_ref[:, j * ublk:(j + 1) * ublk] = jnp.dot(xs_ref[...].astype(BF16), wc_ref[j],
                                                    preferred_element_type=F32)
    o_ref[...] = _glu_out(y_ref[...], u_ref[...].astype(F32), d_ref, wg_ref, bg_ref).astype(o_ref.dtype)


def _s5_sample(u_tm, h0r, h0i, sw, d_skip, w_glu, b_glu, t_new):
    rows, width = u_tm.shape
    ar, ai, _, _, wb, wc = sw
    two_cw = wb.shape[2]
    n_seq, n_state = h0r.shape
    kern = functools.partial(_s5_sample_kernel, t_new=t_new)
    return pl.pallas_call(
        kern,
        out_shape=[jax.ShapeDtypeStruct((rows, width), BF16),
                   jax.ShapeDtypeStruct((n_seq, n_state), F32),
                   jax.ShapeDtypeStruct((n_seq, n_state), F32)],
        scratch_shapes=[pltpu.VMEM((rows, two_cw), F32), pltpu.VMEM((rows, width), F32)],
        compiler_params=pltpu.CompilerParams(vmem_limit_bytes=VMEM_LIMIT),
        name="s5_sample",
    )(u_tm, h0r, h0i, ar, ai, wb, wc, d_skip.reshape(1, -1), w_glu.astype(BF16),
      b_glu.reshape(1, -1))


def _ffn_kernel(te_ref, nu_ref, x_ref, g_ref, wg_ref, wu_ref, wd_ref, o_ref, hn_ref, acc_ref,
                *, residual):
    i, j = pl.program_id(0), pl.program_id(1)

    @pl.when(i < nu_ref[0])
    def _():
        @pl.when(j == 0)
        def _():
            hn_ref[...] = _rms(x_ref[...], g_ref[...]).astype(BF16)
            acc_ref[...] = jnp.zeros_like(acc_ref)

        h = hn_ref[...]
        gate = jnp.dot(h, wg_ref[0].astype(BF16), preferred_element_type=F32)
        up = jnp.dot(h, wu_ref[0].astype(BF16), preferred_element_type=F32)
        act = (gate * jax.nn.sigmoid(gate) * up).astype(BF16)
        acc_ref[...] += jnp.dot(act, wd_ref[0].astype(BF16), preferred_element_type=F32)

        @pl.when(j == pl.num_programs(1) - 1)
        def _():
            o_ref[...] = (x_ref[...] + acc_ref[...]) if residual else acc_ref[...]

    @pl.when((i >= nu_ref[0]) & (j == 0))
    def _():
        o_ref[...] = jnp.zeros_like(o_ref)


def _ffn(x, g, w_gu, w_down, tile_expert, n_used, tm, tf, residual):
    rows, d = x.shape
    d_ff = w_down.shape[1]
    nf = d_ff // tf
    row_map = lambda i, j, te, nu: (jnp.minimum(i, nu[0] - 1), 0)
    jj = lambda i, j, nu: jnp.where(i < nu[0], j, nf - 1)
    kern = functools.partial(_ffn_kernel, residual=residual)
    grid_spec = pltpu.PrefetchScalarGridSpec(
        num_scalar_prefetch=2,
        grid=(rows // tm, nf),
        in_specs=[
            pl.BlockSpec((tm, d), row_map),
            pl.BlockSpec((1, d), lambda i, j, te, nu: (0, 0)),
            pl.BlockSpec((1, d, tf), lambda i, j, te, nu: (te[i], 0, jj(i, j, nu))),
            pl.BlockSpec((1, d, tf), lambda i, j, te, nu: (te[i], 0, nf + jj(i, j, nu))),
            pl.BlockSpec((1, tf, d), lambda i, j, te, nu: (te[i], jj(i, j, nu), 0)),
        ],
        out_specs=pl.BlockSpec((tm, d), lambda i, j, te, nu: (i, 0)),
        scratch_shapes=[pltpu.VMEM((tm, d), BF16), pltpu.VMEM((tm, d), F32)],
    )
    return pl.pallas_call(
        kern,
        grid_spec=grid_spec,
        out_shape=jax.ShapeDtypeStruct((rows, d), F32),
        compiler_params=_params(("arbitrary", "arbitrary")),
        name="swiglu_ffn",
    )(tile_expert, n_used, x, g.reshape(1, d), w_gu, w_gu, w_down)


def _two_source(i, na, a_ref, b_ref):
    return jnp.where(i < na, a_ref[...], b_ref[...])


def _router_kernel(xa_ref, xb_ref, g_ref, wr_ref, eid_ref, rank_ref, gate_ref, cnt_ref,
                   carry_ref, *, na):
    i = pl.program_id(0)
    tr = xa_ref.shape[0]
    n_exp = wr_ref.shape[1]

    @pl.when(i == 0)
    def _():
        carry_ref[...] = jnp.zeros_like(carry_ref)

    h = _rms(_two_source(i, na, xa_ref, xb_ref), g_ref[...])
    logits = jnp.dot(h, wr_ref[...], preferred_element_type=F32, precision=lax.Precision.HIGHEST)
    idx = lax.broadcasted_iota(I32, (tr, n_exp), 1).astype(F32)
    m1 = jnp.max(logits, axis=-1, keepdims=True)
    i1 = jnp.min(jnp.where(logits == m1, idx, float(n_exp)), axis=-1, keepdims=True)
    rest = jnp.where(idx == i1, -jnp.inf, logits)
    m2 = jnp.max(rest, axis=-1, keepdims=True)
    i2 = jnp.min(jnp.where(rest == m2, idx, float(n_exp)), axis=-1, keepdims=True)
    e2 = jnp.exp(m2 - m1)
    den = 1.0 + e2
    oh1 = (idx == i1).astype(F32)
    oh2 = (idx == i2).astype(F32)
    both = oh1 + oh2
    r = lax.broadcasted_iota(I32, (tr, tr), 0)
    c = lax.broadcasted_iota(I32, (tr, tr), 1)
    tri = jnp.where(c < r, 1.0, 0.0).astype(BF16)
    before = jnp.dot(tri, both.astype(BF16), preferred_element_type=F32) + carry_ref[...]
    rank1 = jnp.sum(oh1 * before, axis=-1, keepdims=True)
    rank2 = jnp.sum(oh2 * before, axis=-1, keepdims=True)
    carry_ref[...] += jnp.sum(both, axis=0, keepdims=True)
    k = lax.broadcasted_iota(I32, (tr, 2), 1)
    eid_ref[...] = jnp.where(k == 0, i1, i2).astype(I32)
    rank_ref[...] = jnp.where(k == 0, rank1, rank2).astype(I32)
    gate_ref[...] = jnp.where(k == 0, 1.0 / den, e2 / den)
    cnt_ref[...] = carry_ref[...].astype(I32)


def _router(xa, xb, g, w_router, tr):
    d = xa.shape[1]
    na, nb = xa.shape[0] // tr, xb.shape[0] // tr
    n_exp = w_router.shape[1]
    rows = xa.shape[0] + xb.shape[0]
    amap = lambda i: (jnp.minimum(i, na - 1), 0)
    bmap = lambda i: (jnp.maximum(i - na, 0), 0)
    pair = lambda dt: jax.ShapeDtypeStruct((rows, 2), dt)
    return pl.pallas_call(
        functools.partial(_router_kernel, na=na),
        grid=(na + nb,),
        in_specs=[pl.BlockSpec((tr, d), amap), pl.BlockSpec((tr, d), bmap),
                  pl.BlockSpec((1, d), lambda i: (0, 0)),
                  pl.BlockSpec((d, n_exp), lambda i: (0, 0))],
        out_specs=[pl.BlockSpec((tr, 2), lambda i: (i, 0))] * 3
                  + [pl.BlockSpec((1, n_exp), lambda i: (0, 0))],
        out_shape=[pair(I32), pair(I32), pair(F32), jax.ShapeDtypeStruct((1, n_exp), I32)],
        scratch_shapes=[pltpu.VMEM((1, n_exp), F32)],
        compiler_params=_params(("arbitrary",)),
        name="moe_router",
    )(xa, xb, g.reshape(1, d), w_router)


def _dispatch_kernel(fill_ref, pos_ref, xa_ref, xb_ref, o_ref, zero_ref, sem, zsem, *, na, td, n_exp):
    i = pl.program_id(0)
    tm = zero_ref.shape[0]

    def copy(src_ref, row, dst):
        return pltpu.make_async_copy(src_ref.at[pl.ds(row, 1)], o_ref.at[pl.ds(dst, 1)], sem)

    def issue(src_ref, base):
        def body(r, carry):
            copy(src_ref, base + r, pos_ref[0, 0, 2 * r]).start()
            copy(src_ref, base + r, pos_ref[0, 0, 2 * r + 1]).start()
            return carry
        lax.fori_loop(0, td, body, 0)

    @pl.when(i < na)
    def _():
        issue(xa_ref, i * td)

    @pl.when(i >= na)
    def _():
        issue(xb_ref, (i - na) * td)

    def drain(r, carry):
        copy(xa_ref, 0, 0).wait()
        return carry
    lax.fori_loop(0, 2 * td, drain, 0)

    @pl.when(i == pl.num_programs(0) - 1)
    def _():
        for e in range(n_exp):
            lo, hi = fill_ref[2 * e], fill_ref[2 * e + 1]

            def fill(r, carry):
                copy(xa_ref, 0, r).start()
                return carry
            lax.fori_loop(lo, hi, fill, 0)
            lax.fori_loop(lo, hi, drain, 0)

        zero_ref[...] = jnp.zeros_like(zero_ref)

        def zero_tile(t, carry):
            cp = pltpu.make_async_copy(zero_ref, o_ref.at[pl.ds(pl.multiple_of(t * tm, tm), tm)], zsem)
            cp.start()
            cp.wait()
            return carry
        lax.fori_loop(fill_ref[2 * n_exp], o_ref.shape[0] // tm, zero_tile, 0)


def _dispatch(xa, xb, pos, fill, rows_out, td, tm, n_exp):
    d = xa.shape[1]
    na, nb = xa.shape[0] // td, xb.shape[0] // td
    grid_spec = pltpu.PrefetchScalarGridSpec(
        num_scalar_prefetch=1,
        grid=(na + nb,),
        in_specs=[pl.BlockSpec((1, 1, 2 * td), lambda i, f: (i, 0, 0), memory_space=pltpu.SMEM),
                  pl.BlockSpec(memory_space=pl.ANY), pl.BlockSpec(memory_space=pl.ANY)],
        out_specs=pl.BlockSpec(memory_space=pl.ANY),
        scratch_shapes=[pltpu.VMEM((tm, d), F32), pltpu.SemaphoreType.DMA, pltpu.SemaphoreType.DMA],
    )
    return pl.pallas_call(
        functools.partial(_dispatch_kernel, na=na, td=td, n_exp=n_exp),
        grid_spec=grid_spec,
        out_shape=jax.ShapeDtypeStruct((rows_out, d), F32),
        compiler_params=_params(("arbitrary",)),
        name="moe_dispatch",
    )(fill, pos.reshape(na + nb, 1, 2 * td), xa, xb)


def _combine_kernel(pos_ref, xa_ref, xb_ref, gate_ref, g_ref, y_ref, oa_ref, ob_ref, buf_ref, sem,
                    *, na):
    i = pl.program_id(0)
    tc = xa_ref.shape[0]

    def copy(k, r, src):
        return pltpu.make_async_copy(y_ref.at[pl.ds(src, 1)], buf_ref.at[k, pl.ds(r, 1)], sem)

    def issue(r, carry):
        copy(0, r, pos_ref[0, 0, 2 * r]).start()
        copy(1, r, pos_ref[0, 0, 2 * r + 1]).start()
        return carry
    lax.fori_loop(0, tc, issue, 0)

    def drain(r, carry):
        copy(0, 0, 0).wait()
        return carry
    lax.fori_loop(0, 2 * tc, drain, 0)

    x = _two_source(i, na, xa_ref, xb_ref)
    gate = gate_ref[...]
    out = _rms(x + (gate[:, 0:1] * buf_ref[0] + gate[:, 1:2] * buf_ref[1]), g_ref[...])

    @pl.when(i < na)
    def _():
        oa_ref[...] = out

    @pl.when(i >= na)
    def _():
        ob_ref[...] = out


def _combine(xa, xb, gates, pos, y_sorted, g_final, tc):
    d = xa.shape[1]
    na, nb = xa.shape[0] // tc, xb.shape[0] // tc
    amap = lambda i: (jnp.minimum(i, na - 1), 0)
    bmap = lambda i: (jnp.maximum(i - na, 0), 0)
    return pl.pallas_call(
        functools.partial(_combine_kernel, na=na),
        grid=(na + nb,),
        in_specs=[pl.BlockSpec((1, 1, 2 * tc), lambda i: (i, 0, 0), memory_space=pltpu.SMEM),
                  pl.BlockSpec((tc, d), amap), pl.BlockSpec((tc, d), bmap),
                  pl.BlockSpec((tc, 2), lambda i: (i, 0)),
                  pl.BlockSpec((1, d), lambda i: (0, 0)),
                  pl.BlockSpec(memory_space=pl.ANY)],
        out_specs=[pl.BlockSpec((tc, d), amap), pl.BlockSpec((tc, d), bmap)],
        out_shape=[jax.ShapeDtypeStruct(xa.shape, F32), jax.ShapeDtypeStruct(xb.shape, F32)],
        scratch_shapes=[pltpu.VMEM((2, tc, d), F32), pltpu.SemaphoreType.DMA],
        compiler_params=_params(("arbitrary",)),
        name="moe_combine",
    )(pos.reshape(na + nb, 1, 2 * tc), xa, xb, gates, g_final.reshape(1, d), y_sorted)


def _moe(xa, xb, g_ffn, w_router, w_gu, w_down, g_final, tm, tf):
    n_exp = w_router.shape[1]
    rows = xa.shape[0] + xb.shape[0]
    eid, rank, gates, counts = _router(xa, xb, g_ffn, w_router, tr=256)
    counts = counts[0]
    padded = (counts + tm - 1) // tm * tm
    ends = jnp.cumsum(padded)
    offs = ends - padded
    pos = (offs[eid] + rank).reshape(-1)
    n_tiles = (2 * rows + n_exp * (tm - 1)) // tm
    n_used = (ends[-1] // tm).astype(I32)
    tile_start = jnp.arange(n_tiles, dtype=I32) * tm
    tile_expert = jnp.sum(tile_start[:, None] >= ends[None, :], axis=1).astype(I32)
    last = jnp.take(tile_expert, jnp.maximum(n_used - 1, 0))
    tile_expert = jnp.minimum(jnp.where(tile_start < ends[-1], tile_expert, last), n_exp - 1)
    fill = jnp.concatenate([jnp.stack([offs + counts, offs + padded], axis=1).reshape(-1),
                            n_used.reshape(1)]).astype(I32)
    x_sorted = _dispatch(xa, xb, pos, fill, n_tiles * tm, td=512, tm=tm, n_exp=n_exp)
    y_sorted = _ffn(x_sorted, g_ffn, w_gu, w_down, tile_expert, n_used.reshape(1), tm, tf,
                    residual=False)
    return _combine(xa, xb, gates, pos, y_sorted, g_final, tc=256)


def kernel(x_prompt, x_sample, mem_prompt, cache_swa_k, cache_swa_v, state_ssm_re, state_ssm_im, cache_mem_k, cache_mem_v, g_mix, g_ffn, g_mem, g_final, w_mem_kv, w_in_a, sinks, w_out_a, w_in_b, lam_re, lam_im, log_step, b_re, b_im, c_re, c_im, d_skip, w_glu, b_glu, w_out_b, w_ffn_gu, w_ffn_down, w_router, w_exp_gu, w_exp_down):
    n_batch, seq, d = x_prompt.shape
    n_dec, t_new, _ = x_sample.shape
    n_mem = mem_prompt.shape[1]
    depth = g_mix.shape[0]
    assert depth == 2 and w_in_a.shape[0] == 1 and w_in_b.shape[0] == 1
    n_kv = cache_swa_k.shape[3]
    n_q = sinks.shape[1]
    group = n_q // n_kv
    n_x = cache_mem_k.shape[3]
    x_q = n_x * HEAD_DIM
    kv_a = n_kv * HEAD_DIM
    q_a = n_q * HEAD_DIM
    s5_width = w_glu.shape[1]
    n_groups, n_state_g = lam_re.shape[1:]
    n_state = n_groups * n_state_g
    cache_w = cache_swa_k.shape[2]

    xp = x_prompt.reshape(n_batch * seq, d)
    xs = x_sample.reshape(n_dec * t_new, d)
    mem = mem_prompt.reshape(n_batch * n_mem, d)
    bt = 16
    chunk = 1024
    sub_len = chunk // SUBSEQ

    memkv = [_proj(mem, g_mem[i], w_mem_kv[i], tm=n_mem) for i in range(depth)]
    mem_k_prompt = jnp.stack([m[:, :x_q].reshape(n_batch, n_mem, n_x, HEAD_DIM) for m in memkv])
    mem_v_prompt = jnp.stack([m[:, x_q:].reshape(n_batch, n_mem, n_x, HEAD_DIM) for m in memkv])

    pa_p = _proj(xp, g_mix[0], w_in_a[0], tm=512)
    pa_s = _proj(xs, g_mix[0], w_in_a[0], tm=512)
    x1p = _mixa_prompt(pa_p, xp, memkv[0], sinks[0], w_out_a[0], n_batch, n_kv, group, n_x, n_mem)
    x1s, swa_ks, swa_vs = _mixa_sample(
        pa_s, xs, cache_swa_k[0].reshape(n_dec, cache_w, kv_a), cache_swa_v[0].reshape(n_dec, cache_w, kv_a),
        cache_mem_k[0].reshape(n_dec, n_mem, x_q), cache_mem_v[0].reshape(n_dec, n_mem, x_q),
        sinks[0], w_out_a[0], n_kv, group, n_x, t_new, bt)
    w_keep = min(WINDOW, seq)
    kv_p = pa_p.reshape(n_batch, seq, -1)[:, seq - w_keep:, q_a:q_a + 2 * kv_a]
    swa_k_prompt = kv_p[..., :kv_a].reshape(1, n_batch, w_keep, n_kv, HEAD_DIM)
    swa_v_prompt = kv_p[..., kv_a:].reshape(1, n_batch, w_keep, n_kv, HEAD_DIM)
    swa_k_sample = swa_ks.reshape(1, n_dec, cache_w, n_kv, HEAD_DIM)
    swa_v_sample = swa_vs.reshape(1, n_dec, cache_w, n_kv, HEAD_DIM)

    x2p = _ffn(x1p, g_ffn[0], w_ffn_gu, w_ffn_down, jnp.zeros((x1p.shape[0] // 1024,), I32),
               jnp.full((1,), x1p.shape[0] // 1024, I32), 1024, 512, residual=True)
    x2s = _ffn(x1s, g_ffn[0], w_ffn_gu, w_ffn_down, jnp.zeros((x1s.shape[0] // 1024,), I32),
               jnp.full((1,), x1s.shape[0] // 1024, I32), 1024, 512, residual=True)

    pb_p = _proj(x2p, g_mix[1], w_in_b[0], tm=512)
    pb_s = _proj(x2s, g_mix[1], w_in_b[0], tm=512)
    sw = _s5_weights(lam_re[0], lam_im[0], log_step[0], b_re[0], b_im[0], c_re[0], c_im[0], sub_len)
    nchunk = seq // chunk
    u_perm = (pb_p[:, :s5_width].astype(BF16)
              .reshape(n_batch, nchunk, SUBSEQ, sub_len, s5_width)
              .transpose(0, 1, 3, 2, 4).reshape(n_batch * seq, s5_width))
    so_perm, h_fin = _s5_prompt(u_perm, sw, d_skip[0], w_glu[0], b_glu[0], n_batch, chunk)
    so_p = (so_perm.reshape(n_batch, nchunk, sub_len, SUBSEQ, s5_width)
            .transpose(0, 1, 3, 2, 4).reshape(n_batch * seq, s5_width))
    ssm_re_prompt = h_fin[:, 0, :n_state].reshape(1, n_batch, n_groups, n_state_g)
    ssm_im_prompt = h_fin[:, 0, n_state:].reshape(1, n_batch, n_groups, n_state_g)

    u_tm = (pb_s[:, :s5_width].astype(BF16).reshape(n_dec, t_new, s5_width)
            .transpose(1, 0, 2).reshape(n_dec * t_new, s5_width))
    so_tm, hr_s, hi_s = _s5_sample(u_tm, state_ssm_re[0].reshape(n_dec, n_state),
                                   state_ssm_im[0].reshape(n_dec, n_state), sw,
                                   d_skip[0], w_glu[0], b_glu[0], t_new)
    so_s = so_tm.reshape(t_new, n_dec, s5_width).transpose(1, 0, 2).reshape(n_dec * t_new, s5_width)
    ssm_re_sample = hr_s.reshape(1, n_dec, n_groups, n_state_g)
    ssm_im_sample = hi_s.reshape(1, n_dec, n_groups, n_state_g)

    x3p = _tailb_prompt(so_p, pb_p, x2p, memkv[1], w_out_b[0], n_batch, n_x, n_mem, tm=256)
    x3s = _tailb_sample(so_s, pb_s, x2s, cache_mem_k[1].reshape(n_dec, n_mem, x_q),
                        cache_mem_v[1].reshape(n_dec, n_mem, x_q), w_out_b[0], n_x, t_new, bt)

    yp, ys = _moe(x3p, x3s, g_ffn[1], w_router[0], w_exp_gu[0], w_exp_down[0], g_final, 1024, 512)
    return (yp.reshape(n_batch, seq, d), ys.reshape(n_dec, t_new, d),
            swa_k_prompt, swa_v_prompt, ssm_re_prompt, ssm_im_prompt,
            mem_k_prompt, mem_v_prompt, swa_k_sample, swa_v_sample,
            ssm_re_sample, ssm_im_sample)
```

```python
import functools

import jax
import jax.numpy as jnp
from jax import lax
from jax.experimental import pallas as pl
from jax.experimental.pallas import tpu as pltpu

F32 = jnp.float32
BF16 = jnp.bfloat16
I32 = jnp.int32

HEAD_DIM = 64
WINDOW = 128
BLOCK = 128
EPS = 1e-5
SCALE = HEAD_DIM ** -0.5
S5_GROUP = 16
SUBSEQ = 8
SUBLANES = 8
LANE_WIN = 512
VMEM_LIMIT = 56 * 1024 * 1024

_NT = (((1,), (1,)), ((), ()))


def _params(sem, vmem=VMEM_LIMIT):
    return pltpu.CompilerParams(dimension_semantics=sem, vmem_limit_bytes=vmem)


def _rms(x, g):
    return x * lax.rsqrt(jnp.mean(x * x, axis=-1, keepdims=True) + EPS) * g


def _softmax(s):
    m = jnp.max(s, axis=-1, keepdims=True)
    p = jnp.exp(s - m)
    return p / jnp.sum(p, axis=-1, keepdims=True)


def _proj_kernel(x_ref, g_ref, w_ref, o_ref):
    h = _rms(x_ref[...], g_ref[...]).astype(BF16)
    o_ref[...] = jnp.dot(h, w_ref[...], preferred_element_type=F32)


def _proj(x, g, w, tm):
    rows, d = x.shape
    n = w.shape[1]
    return pl.pallas_call(
        _proj_kernel,
        grid=(rows // tm,),
        in_specs=[pl.BlockSpec((tm, d), lambda i: (i, 0)),
                  pl.BlockSpec((1, d), lambda i: (0, 0)),
                  pl.BlockSpec((d, n), lambda i: (0, 0))],
        out_specs=pl.BlockSpec((tm, n), lambda i: (i, 0)),
        out_shape=jax.ShapeDtypeStruct((rows, n), F32),
        compiler_params=_params(("parallel",)),
        name="rms_proj",
    )(x, g.reshape(1, d), w.astype(BF16))


def _sink_attention_group(q_rows, k_h, v_h, mask, sink_col):
    s = lax.dot_general(q_rows, k_h, _NT, preferred_element_type=F32)
    s = jnp.where(mask, s, -jnp.inf)
    m = jnp.maximum(jnp.max(s, axis=-1, keepdims=True), sink_col)
    p = jnp.exp(s - m)
    den = jnp.sum(p, axis=-1, keepdims=True) + jnp.exp(sink_col - m)
    return jnp.dot((p / den).astype(BF16), v_h, preferred_element_type=F32)


def _cross_attention(xq, mk, mv, n_heads, mix_ref, rows, col0):
    xq = (xq * SCALE).astype(BF16)
    for h in range(n_heads):
        sl = slice(h * HEAD_DIM, (h + 1) * HEAD_DIM)
        s = lax.dot_general(xq[:, sl], mk[:, sl], _NT, preferred_element_type=F32)
        o = jnp.dot(_softmax(s).astype(BF16), mv[:, sl], preferred_element_type=F32)
        mix_ref[rows, col0 + h * HEAD_DIM:col0 + (h + 1) * HEAD_DIM] = o


def _sink_column(sink_ref, heads, rows_per_head):
    n = len(heads)
    row = lax.broadcasted_iota(I32, (n * rows_per_head, 1), 0)
    col = jnp.full((n * rows_per_head, 1), sink_ref[0, heads[n - 1]], F32)
    for g in range(n - 2, -1, -1):
        col = jnp.where(row < (g + 1) * rows_per_head, sink_ref[0, heads[g]], col)
    return col


def _head_mask(n_heads, t):
    shape = (n_heads * t, n_heads * HEAD_DIM)
    row = lax.broadcasted_iota(I32, shape, 0)
    col = lax.broadcasted_iota(I32, shape, 1)
    lo = (row // t) * HEAD_DIM
    return (col >= lo) & (col < lo + HEAD_DIM)


def _spread_heads(x, hmask, n_heads):
    return jnp.where(hmask, jnp.concatenate([x] * n_heads, axis=0), 0.0).astype(BF16)


def _gather_heads(o, hmask, n_heads, t):
    o = jnp.where(hmask, o, 0.0)
    acc = o[0:t]
    for h in range(1, n_heads):
        acc = acc + o[h * t:(h + 1) * t]
    return acc


def _cross_attention_small(xq, mk, mv, hmask, n_heads):
    t = xq.shape[0]
    s = lax.dot_general(_spread_heads(xq * SCALE, hmask, n_heads), mk, _NT,
                        preferred_element_type=F32)
    o = jnp.dot(_softmax(s).astype(BF16), mv, preferred_element_type=F32)
    return _gather_heads(o, hmask, n_heads, t)


def _out_proj(x_ref, mix_ref, wo_ref, o_ref):
    o_ref[...] = x_ref[...] + jnp.dot(mix_ref[...].astype(BF16), wo_ref[...],
                                      preferred_element_type=F32)


def _mixa_prompt_kernel(sink_ref, q_ref, kp_ref, kc_ref, vp_ref, vc_ref, xq_ref, x_ref,
                        mk_ref, mv_ref, wo_ref, o_ref, mix_ref, *, n_kv, group, n_x):
    i = pl.program_id(1)
    q = q_ref[...] * SCALE
    kk = jnp.concatenate([kp_ref[...], kc_ref[...]], axis=0).astype(BF16)
    vv = jnp.concatenate([vp_ref[...], vc_ref[...]], axis=0).astype(BF16)
    m_rows = group * BLOCK
    qi = lax.rem(lax.broadcasted_iota(I32, (m_rows, 2 * BLOCK), 0), BLOCK)
    kj = lax.broadcasted_iota(I32, (m_rows, 2 * BLOCK), 1)
    dist = qi + BLOCK - kj
    first = jnp.where(i > 0, 0, BLOCK)
    mask = (dist >= 0) & (dist <= WINDOW) & (kj >= first)
    for h in range(n_kv):
        sl = slice(h * HEAD_DIM, (h + 1) * HEAD_DIM)
        cols = [(g * n_kv + h) * HEAD_DIM for g in range(group)]
        qh = jnp.concatenate([q[:, c0:c0 + HEAD_DIM] for c0 in cols], axis=0).astype(BF16)
        o = _sink_attention_group(qh, kk[:, sl], vv[:, sl], mask,
                                  _sink_column(sink_ref, [g * n_kv + h for g in range(group)], BLOCK))
        for g, c0 in enumerate(cols):
            mix_ref[:, c0:c0 + HEAD_DIM] = o[g * BLOCK:(g + 1) * BLOCK]
    q_a = n_kv * group * HEAD_DIM
    _cross_attention(xq_ref[...], mk_ref[...].astype(BF16), mv_ref[...].astype(BF16), n_x,
                     mix_ref, slice(None), q_a)
    _out_proj(x_ref, mix_ref, wo_ref, o_ref)


def _mixa_prompt(p, x, memkv, sinks, w_out, n_batch, n_kv, group, n_x, n_mem):
    rows, d = x.shape
    nb = rows // n_batch // BLOCK
    q_a = n_kv * group * HEAD_DIM
    kv_a = n_kv * HEAD_DIM
    x_q = n_x * HEAD_DIM
    kcol = q_a // kv_a
    vcol = (q_a + kv_a) // kv_a
    xcol = (q_a + 2 * kv_a) // x_q
    cur = lambda b, i: b * nb + i
    prev = lambda b, i: b * nb + jnp.maximum(i - 1, 0)
    kern = functools.partial(_mixa_prompt_kernel, n_kv=n_kv, group=group, n_x=n_x)
    return pl.pallas_call(
        kern,
        grid=(n_batch, nb),
        in_specs=[
            pl.BlockSpec(memory_space=pltpu.SMEM),
            pl.BlockSpec((BLOCK, q_a), lambda b, i: (cur(b, i), 0)),
            pl.BlockSpec((BLOCK, kv_a), lambda b, i: (prev(b, i), kcol)),
            pl.BlockSpec((BLOCK, kv_a), lambda b, i: (cur(b, i), kcol)),
            pl.BlockSpec((BLOCK, kv_a), lambda b, i: (prev(b, i), vcol)),
            pl.BlockSpec((BLOCK, kv_a), lambda b, i: (cur(b, i), vcol)),
            pl.BlockSpec((BLOCK, x_q), lambda b, i: (cur(b, i), xcol)),
            pl.BlockSpec((BLOCK, d), lambda b, i: (cur(b, i), 0)),
            pl.BlockSpec((n_mem, x_q), lambda b, i: (b, 0)),
            pl.BlockSpec((n_mem, x_q), lambda b, i: (b, 1)),
            pl.BlockSpec((q_a + x_q, d), lambda b, i: (0, 0)),
        ],
        out_specs=pl.BlockSpec((BLOCK, d), lambda b, i: (cur(b, i), 0)),
        out_shape=jax.ShapeDtypeStruct((rows, d), F32),
        scratch_shapes=[pltpu.VMEM((BLOCK, q_a + x_q), F32)],
        compiler_params=_params(("parallel", "parallel")),
        name="mix_a_prompt",
    )(sinks.reshape(1, -1), p, p, p, p, p, p, x, memkv, memkv, w_out.astype(BF16))


def _mixa_sample_kernel(sink_ref, p_ref, x_ref, ck_ref, cv_ref, mk_ref, mv_ref, wo_ref,
                        o_ref, ok_ref, ov_ref, mix_ref, *, n_kv, group, n_x, t_new, bt):
    w = ck_ref.shape[1]
    q_a = n_kv * group * HEAD_DIM
    kv_a = n_kv * HEAD_DIM
    m_rows = group * n_kv * t_new
    ti = lax.rem(lax.broadcasted_iota(I32, (m_rows, w + t_new), 0), t_new)
    kj = lax.broadcasted_iota(I32, (m_rows, w + t_new), 1)
    dist = ti + w - kj
    mask = (dist >= 0) & (dist <= WINDOW)
    kv_mask = _head_mask(n_kv, t_new)
    x_mask = _head_mask(n_x, t_new)
    sink_col = sink_ref[...]

    def body(bb, carry):
        rows = pl.ds(pl.multiple_of(bb * t_new, t_new), t_new)
        q = p_ref[rows, 0:q_a] * SCALE
        k_new = p_ref[rows, q_a:q_a + kv_a]
        v_new = p_ref[rows, q_a + kv_a:q_a + 2 * kv_a]
        xq = p_ref[rows, q_a + 2 * kv_a:]
        k_old = ck_ref[bb]
        v_old = cv_ref[bb]
        ok_ref[bb, 0:w - t_new, :] = k_old[t_new:, :]
        ok_ref[bb, w - t_new:w, :] = k_new
        ov_ref[bb, 0:w - t_new, :] = v_old[t_new:, :]
        ov_ref[bb, w - t_new:w, :] = v_new
        kk = jnp.concatenate([k_old, k_new], axis=0).astype(BF16)
        vv = jnp.concatenate([v_old, v_new], axis=0).astype(BF16)
        qs = jnp.concatenate([_spread_heads(q[:, g * kv_a:(g + 1) * kv_a], kv_mask, n_kv)
                              for g in range(group)], axis=0)
        o = _sink_attention_group(qs, kk, vv, mask, sink_col)
        blk = n_kv * t_new
        for g in range(group):
            mix_ref[rows, g * kv_a:(g + 1) * kv_a] = _gather_heads(
                o[g * blk:(g + 1) * blk], kv_mask, n_kv, t_new)
        mix_ref[rows, q_a:] = _cross_attention_small(
            xq, mk_ref[bb].astype(BF16), mv_ref[bb].astype(BF16), x_mask, n_x)
        return carry

    lax.fori_loop(0, bt, body, 0, unroll=2)
    _out_proj(x_ref, mix_ref, wo_ref, o_ref)


def _mixa_sample(p, x, cache_k, cache_v, mem_k, mem_v, sinks, w_out, n_kv, group, n_x, t_new, bt):
    rows, d = x.shape
    n_dec, w, kv_a = cache_k.shape
    n_mem, x_q = mem_k.shape[1:]
    q_a = n_kv * group * HEAD_DIM
    tm = bt * t_new
    kern = functools.partial(_mixa_sample_kernel, n_kv=n_kv, group=group, n_x=n_x,
                             t_new=t_new, bt=bt)
    return pl.pallas_call(
        kern,
        grid=(n_dec // bt,),
        in_specs=[
            pl.BlockSpec((n_kv * group * t_new, 1), lambda i: (0, 0)),
            pl.BlockSpec((tm, p.shape[1]), lambda i: (i, 0)),
            pl.BlockSpec((tm, d), lambda i: (i, 0)),
            pl.BlockSpec((bt, w, kv_a), lambda i: (i, 0, 0)),
            pl.BlockSpec((bt, w, kv_a), lambda i: (i, 0, 0)),
            pl.BlockSpec((bt, n_mem, x_q), lambda i: (i, 0, 0)),
            pl.BlockSpec((bt, n_mem, x_q), lambda i: (i, 0, 0)),
            pl.BlockSpec((q_a + x_q, d), lambda i: (0, 0)),
        ],
        out_specs=[pl.BlockSpec((tm, d), lambda i: (i, 0)),
                   pl.BlockSpec((bt, w, kv_a), lambda i: (i, 0, 0)),
                   pl.BlockSpec((bt, w, kv_a), lambda i: (i, 0, 0))],
        out_shape=[jax.ShapeDtypeStruct((rows, d), F32),
                   jax.ShapeDtypeStruct((n_dec, w, kv_a), F32),
                   jax.ShapeDtypeStruct((n_dec, w, kv_a), F32)],
        scratch_shapes=[pltpu.VMEM((tm, q_a + x_q), F32)],
        compiler_params=_params(("parallel",)),
        name="mix_a_sample",
    )(jnp.repeat(sinks, t_new).reshape(-1, 1), p, x, cache_k, cache_v, mem_k, mem_v,
      w_out.astype(BF16))


def _tailb_prompt_kernel(s_ref, xq_ref, x_ref, mk_ref, mv_ref, wo_ref, o_ref, mix_ref, *, n_x):
    width = s_ref.shape[1]
    mix_ref[:, 0:width] = s_ref[...].astype(F32)
    _cross_attention(xq_ref[...], mk_ref[...].astype(BF16), mv_ref[...].astype(BF16), n_x,
                     mix_ref, slice(None), width)
    _out_proj(x_ref, mix_ref, wo_ref, o_ref)


def _tailb_prompt(s_out, p, x, memkv, w_out, n_batch, n_x, n_mem, tm):
    rows, d = x.shape
    width = s_out.shape[1]
    x_q = n_x * HEAD_DIM
    per_b = rows // n_batch // tm
    kern = functools.partial(_tailb_prompt_kernel, n_x=n_x)
    return pl.pallas_call(
        kern,
        grid=(rows // tm,),
        in_specs=[
            pl.BlockSpec((tm, width), lambda i: (i, 0)),
            pl.BlockSpec((tm, x_q), lambda i: (i, width // x_q)),
            pl.BlockSpec((tm, d), lambda i: (i, 0)),
            pl.BlockSpec((n_mem, x_q), lambda i: (i // per_b, 0)),
            pl.BlockSpec((n_mem, x_q), lambda i: (i // per_b, 1)),
            pl.BlockSpec((width + x_q, d), lambda i: (0, 0)),
        ],
        out_specs=pl.BlockSpec((tm, d), lambda i: (i, 0)),
        out_shape=jax.ShapeDtypeStruct((rows, d), F32),
        scratch_shapes=[pltpu.VMEM((tm, width + x_q), F32)],
        compiler_params=_params(("parallel",)),
        name="tail_b_prompt",
    )(s_out, p, x, memkv, memkv, w_out.astype(BF16))


def _tailb_sample_kernel(s_ref, xq_ref, x_ref, mk_ref, mv_ref, wo_ref, o_ref, mix_ref,
                         *, n_x, t_new, bt):
    width = s_ref.shape[1]
    mix_ref[:, 0:width] = s_ref[...].astype(F32)

    x_mask = _head_mask(n_x, t_new)

    def body(bb, carry):
        rows = pl.ds(pl.multiple_of(bb * t_new, t_new), t_new)
        mix_ref[rows, width:] = _cross_attention_small(
            xq_ref[rows, :], mk_ref[bb].astype(BF16), mv_ref[bb].astype(BF16), x_mask, n_x)
        return carry

    lax.fori_loop(0, bt, body, 0, unroll=2)
    _out_proj(x_ref, mix_ref, wo_ref, o_ref)


def _tailb_sample(s_out, p, x, mem_k, mem_v, w_out, n_x, t_new, bt):
    rows, d = x.shape
    width = s_out.shape[1]
    n_dec, n_mem, x_q = mem_k.shape
    tm = bt * t_new
    kern = functools.partial(_tailb_sample_kernel, n_x=n_x, t_new=t_new, bt=bt)
    return pl.pallas_call(
        kern,
        grid=(n_dec // bt,),
        in_specs=[
            pl.BlockSpec((tm, width), lambda i: (i, 0)),
            pl.BlockSpec((tm, x_q), lambda i: (i, width // x_q)),
            pl.BlockSpec((tm, d), lambda i: (i, 0)),
            pl.BlockSpec((bt, n_mem, x_q), lambda i: (i, 0, 0)),
            pl.BlockSpec((bt, n_mem, x_q), lambda i: (i, 0, 0)),
            pl.BlockSpec((width + x_q, d), lambda i: (0, 0)),
        ],
        out_specs=pl.BlockSpec((tm, d), lambda i: (i, 0)),
        out_shape=jax.ShapeDtypeStruct((rows, d), F32),
        scratch_shapes=[pltpu.VMEM((tm, width + x_q), F32)],
        compiler_params=_params(("parallel",)),
        name="tail_b_sample",
    )(s_out, p, x, mem_k, mem_v, w_out.astype(BF16))


def _s5_disc_kernel(lr_ref, li_ref, ls_ref, br_ref, bi_ref,
                    ar_ref, ai_ref, apr_ref, api_ref, bbr_ref, bbi_ref, *, n_square):
    lr, li = lr_ref[...], li_ref[...]
    step = jnp.exp(ls_ref[...])
    mag = jnp.exp(lr * step)
    ar, ai = mag * jnp.cos(li * step), mag * jnp.sin(li * step)
    den = lr * lr + li * li
    cr = ((ar - 1.0) * lr + ai * li) / den
    ci = (ai * lr - (ar - 1.0) * li) / den
    br, bi = br_ref[...], bi_ref[...]
    bbr_ref[...] = cr * br - ci * bi
    bbi_ref[...] = cr * bi + ci * br
    ar_ref[...] = ar
    ai_ref[...] = ai
    pr, pi = ar, ai
    for _ in range(n_square):
        pr, pi = pr * pr - pi * pi, 2.0 * pr * pi
    apr_ref[...] = pr
    api_ref[...] = pi


def _s5_weights(lam_re, lam_im, log_step, b_re, b_im, c_re, c_im, sub_len):
    g, p = lam_re.shape
    hh = b_re.shape[2]
    rep = lambda a: jnp.repeat(a, hh, axis=0)
    flat = lambda a: a.transpose(0, 2, 1).reshape(g * hh, p)
    n_square = sub_len.bit_length() - 1
    assert 1 << n_square == sub_len
    shp = jax.ShapeDtypeStruct((g * hh, p), F32)
    ar, ai, apr, api, bbr, bbi = pl.pallas_call(
        functools.partial(_s5_disc_kernel, n_square=n_square),
        out_shape=[shp] * 6,
        name="s5_discretise",
    )(rep(lam_re), rep(lam_im), rep(jnp.broadcast_to(log_step[:, None], (g, p))),
      flat(b_re), flat(b_im))
    row = lambda a: a[::hh].reshape(1, g * p)
    gb = 256 // hh
    nblk = g // gb
    eye = jnp.eye(gb, dtype=F32)
    bd_in = lambda a: (a.reshape(nblk, gb, hh, 1, p) * eye[None, :, None, :, None]
                       ).reshape(nblk, gb * hh, gb * p)
    wb = jnp.concatenate([bd_in(bbr), bd_in(bbi)], axis=-1).astype(BF16)
    bd_out = lambda c: (c.reshape(nblk, gb, hh, p).transpose(0, 1, 3, 2)[:, :, :, None, :]
                        * eye[None, :, None, :, None]).reshape(nblk, gb * p, gb * hh)
    wc = jnp.concatenate([bd_out(c_re), -bd_out(c_im)], axis=1).astype(BF16)
    return row(ar), row(ai), row(apr), row(api), wb, wc


def _cstep(ar, ai, hr, hi, xr, xi):
    return ar * hr - ai * hi + xr, ar * hi + ai * hr + xi


def _glu_out(y, u, d_ref, wg_ref, bg_ref):
    z = jax.nn.gelu(y + d_ref[...] * u)
    gate = jnp.dot(z.astype(BF16), wg_ref[...], preferred_element_type=F32) + bg_ref[...]
    return z * jax.nn.sigmoid(gate)


def _s5_prompt_kernel(u_ref, ar_ref, ai_ref, apr_ref, api_ref, wb_ref, wc_ref, d_ref, wg_ref,
                      bg_ref, o_ref, hf_ref, xs_ref, y_ref, carry_ref, *, n_steps):
    c = pl.program_id(1)
    nblk, ublk, two_cw = wb_ref.shape
    cw = two_cw // 2
    n_state = nblk * cw

    @pl.when(c == 0)
    def _():
        carry_ref[...] = jnp.zeros_like(carry_ref)

    row8 = lax.broadcasted_iota(I32, (SUBLANES, LANE_WIN), 0)
    for j in range(nblk):
        xs_ref[...] = jnp.dot(u_ref[:, j * ublk:(j + 1) * ublk], wb_ref[j],
                              preferred_element_type=F32)
        for lb in range(cw // LANE_WIN):
            re = slice(lb * LANE_WIN, (lb + 1) * LANE_WIN)
            im = slice(cw + lb * LANE_WIN, cw + (lb + 1) * LANE_WIN)
            gre = slice(j * cw + lb * LANE_WIN, j * cw + (lb + 1) * LANE_WIN)
            gim = slice(n_state + j * cw + lb * LANE_WIN, n_state + j * cw + (lb + 1) * LANE_WIN)
            bc = lambda r: jnp.broadcast_to(r[:, gre], (SUBLANES, LANE_WIN))
            ar, ai, apr, api = bc(ar_ref), bc(ai_ref), bc(apr_ref), bc(api_ref)
            c_r = jnp.broadcast_to(carry_ref[:, gre], (SUBLANES, LANE_WIN))
            c_i = jnp.broadcast_to(carry_ref[:, gim], (SUBLANES, LANE_WIN))

            def rows(i):
                return pl.ds(pl.multiple_of(i * SUBLANES, SUBLANES), SUBLANES)

            def pass1(i, h):
                return _cstep(ar, ai, h[0], h[1], xs_ref[rows(i), re], xs_ref[rows(i), im])

            zero = jnp.zeros((SUBLANES, LANE_WIN), F32)
            e_r, e_i = lax.fori_loop(0, n_steps, pass1, (zero, zero), unroll=4)
            s_r = jnp.where(row8 == 0, c_r, 0.0)
            s_i = jnp.where(row8 == 0, c_i, 0.0)
            for _ in range(SUBSEQ - 1):
                n_r, n_i = _cstep(apr, api, s_r, s_i, e_r, e_i)
                s_r = jnp.where(row8 == 0, c_r, pltpu.roll(n_r, 1, 0))
                s_i = jnp.where(row8 == 0, c_i, pltpu.roll(n_i, 1, 0))
            n_r, n_i = _cstep(apr, api, s_r, s_i, e_r, e_i)
            carry_ref[:, gre] = n_r[SUBSEQ - 1:SUBSEQ, :]
            carry_ref[:, gim] = n_i[SUBSEQ - 1:SUBSEQ, :]

            def pass2(i, h):
                h = _cstep(ar, ai, h[0], h[1], xs_ref[rows(i), re], xs_ref[rows(i), im])
                xs_ref[rows(i), re] = h[0]
                xs_ref[rows(i), im] = h[1]
                return h

            lax.fori_loop(0, n_steps, pass2, (s_r, s_i), unroll=4)
        y_ref[:, j * ublk:(j + 1) * ublk] = jnp.dot(xs_ref[...].astype(BF16), wc_ref[j],
                                                    preferred_element_type=F32)
    o_ref[...] = _glu_out(y_ref[...], u_ref[...].astype(F32), d_ref, wg_ref, bg_ref).astype(o_ref.dtype)
    hf_ref[0] = carry_ref[...]


def _s5_prompt(u_perm, sw, d_skip, w_glu, b_glu, n_batch, chunk):
    rows, width = u_perm.shape
    ar, ai, apr, api, wb, wc = sw
    nblk, ublk, two_cw = wb.shape
    n_state = ar.shape[1]
    nchunk = rows // n_batch // chunk
    const2 = lambda b, c: (0, 0)
    const3 = lambda b, c: (0, 0, 0)
    kern = functools.partial(_s5_prompt_kernel, n_steps=chunk // SUBSEQ)
    return pl.pallas_call(
        kern,
        grid=(n_batch, nchunk),
        in_specs=[
            pl.BlockSpec((chunk, width), lambda b, c: (b * nchunk + c, 0)),
            pl.BlockSpec((1, n_state), const2), pl.BlockSpec((1, n_state), const2),
            pl.BlockSpec((1, n_state), const2), pl.BlockSpec((1, n_state), const2),
            pl.BlockSpec(wb.shape, const3), pl.BlockSpec(wc.shape, const3),
            pl.BlockSpec((1, width), const2), pl.BlockSpec((width, width), const2),
            pl.BlockSpec((1, width), const2),
        ],
        out_specs=[pl.BlockSpec((chunk, width), lambda b, c: (b * nchunk + c, 0)),
                   pl.BlockSpec((1, 1, 2 * n_state), lambda b, c: (b, 0, 0))],
        out_shape=[jax.ShapeDtypeStruct((rows, width), BF16),
                   jax.ShapeDtypeStruct((n_batch, 1, 2 * n_state), F32)],
        scratch_shapes=[pltpu.VMEM((chunk, two_cw), F32), pltpu.VMEM((chunk, width), F32),
                        pltpu.VMEM((1, 2 * n_state), F32)],
        compiler_params=_params(("arbitrary", "arbitrary")),
        name="s5_prompt",
    )(u_perm, ar, ai, apr, api, wb, wc, d_skip.reshape(1, -1), w_glu.astype(BF16),
      b_glu.reshape(1, -1))


def _s5_sample_kernel(u_ref, h0r_ref, h0i_ref, ar_ref, ai_ref, wb_ref, wc_ref, d_ref, wg_ref,
                      bg_ref, o_ref, hr_ref, hi_ref, xs_ref, y_ref, *, t_new):
    nblk, ublk, two_cw = wb_ref.shape
    cw = two_cw // 2
    n_seq = h0r_ref.shape[0]
    for j in range(nblk):
        xs_ref[...] = jnp.dot(u_ref[:, j * ublk:(j + 1) * ublk], wb_ref[j],
                              preferred_element_type=F32)
        for lb in range(cw // LANE_WIN):
            re = slice(lb * LANE_WIN, (lb + 1) * LANE_WIN)
            im = slice(cw + lb * LANE_WIN, cw + (lb + 1) * LANE_WIN)
            gre = slice(j * cw + lb * LANE_WIN, j * cw + (lb + 1) * LANE_WIN)
            ar = jnp.broadcast_to(ar_ref[:, gre], (SUBLANES, LANE_WIN))
            ai = jnp.broadcast_to(ai_ref[:, gre], (SUBLANES, LANE_WIN))

            def group_body(rg, carry):
                base = pl.multiple_of(rg * SUBLANES, SUBLANES)
                h = (h0r_ref[pl.ds(base, SUBLANES), gre], h0i_ref[pl.ds(base, SUBLANES), gre])
                for t in range(t_new):
                    r = pl.ds(t * n_seq + base, SUBLANES)
                    h = _cstep(ar, ai, h[0], h[1], xs_ref[r, re], xs_ref[r, im])
                    xs_ref[r, re] = h[0]
                    xs_ref[r, im] = h[1]
                hr_ref[pl.ds(base, SUBLANES), gre] = h[0]
                hi_ref[pl.ds(base, SUBLANES), gre] = h[1]
                return carry

            lax.fori_loop(0, n_seq // SUBLANES, group_body, 0)
        y_ref[:, j * ublk:(j + 1) * ublk] = jnp.dot(xs_ref[...].astype(BF16), wc_ref[j],
                                                    preferred_element_type=F32)
    o_ref[...] = _glu_out(y_ref[...], u_ref[...].astype(F32), d_ref, wg_ref, bg_ref).astype(o_ref.dtype)


def _s5_sample(u_tm, h0r, h0i, sw, d_skip, w_glu, b_glu, t_new):
    rows, width = u_tm.shape
    ar, ai, _, _, wb, wc = sw
    two_cw = wb.shape[2]
    n_seq, n_state = h0r.shape
    kern = functools.partial(_s5_sample_kernel, t_new=t_new)
    return pl.pallas_call(
        kern,
        out_shape=[jax.ShapeDtypeStruct((rows, width), BF16),
                   jax.ShapeDtypeStruct((n_seq, n_state), F32),
                   jax.ShapeDtypeStruct((n_seq, n_state), F32)],
        scratch_shapes=[pltpu.VMEM((rows, two_cw), F32), pltpu.VMEM((rows, width), F32)],
        compiler_params=pltpu.CompilerParams(vmem_limit_bytes=VMEM_LIMIT),
        name="s5_sample",
    )(u_tm, h0r, h0i, ar, ai, wb, wc, d_skip.reshape(1, -1), w_glu.astype(BF16),
      b_glu.reshape(1, -1))


def _ffn_kernel(te_ref, nu_ref, x_ref, g_ref, wg_ref, wu_ref, wd_ref, o_ref, hn_ref, acc_ref,
                *, residual):
    i, j = pl.program_id(0), pl.program_id(1)

    @pl.when(i < nu_ref[0])
    def _():
        @pl.when(j == 0)
        def _():
            hn_ref[...] = _rms(x_ref[...], g_ref[...]).astype(BF16)
            acc_ref[...] = jnp.zeros_like(acc_ref)

        h = hn_ref[...]
        gate = jnp.dot(h, wg_ref[0].astype(BF16), preferred_element_type=F32)
        up = jnp.dot(h, wu_ref[0].astype(BF16), preferred_element_type=F32)
        act = (gate * jax.nn.sigmoid(gate) * up).astype(BF16)
        acc_ref[...] += jnp.dot(act, wd_ref[0].astype(BF16), preferred_element_type=F32)

        @pl.when(j == pl.num_programs(1) - 1)
        def _():
            o_ref[...] = (x_ref[...] + acc_ref[...]) if residual else acc_ref[...]

    @pl.when((i >= nu_ref[0]) & (j == 0))
    def _():
        o_ref[...] = jnp.zeros_like(o_ref)


def _ffn(x, g, w_gu, w_down, tile_expert, n_used, tm, tf, residual):
    rows, d = x.shape
    d_ff = w_down.shape[1]
    nf = d_ff // tf
    row_map = lambda i, j, te, nu: (jnp.minimum(i, nu[0] - 1), 0)
    jj = lambda i, j, nu: jnp.where(i < nu[0], j, nf - 1)
    kern = functools.partial(_ffn_kernel, residual=residual)
    grid_spec = pltpu.PrefetchScalarGridSpec(
        num_scalar_prefetch=2,
        grid=(rows // tm, nf),
        in_specs=[
            pl.BlockSpec((tm, d), row_map),
            pl.BlockSpec((1, d), lambda i, j, te, nu: (0, 0)),
            pl.BlockSpec((1, d, tf), lambda i, j, te, nu: (te[i], 0, jj(i, j, nu))),
            pl.BlockSpec((1, d, tf), lambda i, j, te, nu: (te[i], 0, nf + jj(i, j, nu))),
            pl.BlockSpec((1, tf, d), lambda i, j, te, nu: (te[i], jj(i, j, nu), 0)),
        ],
        out_specs=pl.BlockSpec((tm, d), lambda i, j, te, nu: (i, 0)),
        scratch_shapes=[pltpu.VMEM((tm, d), BF16), pltpu.VMEM((tm, d), F32)],
    )
    return pl.pallas_call(
        kern,
        grid_spec=grid_spec,
        out_shape=jax.ShapeDtypeStruct((rows, d), F32),
        compiler_params=_params(("arbitrary", "arbitrary")),
        name="swiglu_ffn",
    )(tile_expert, n_used, x, g.reshape(1, d), w_gu, w_gu, w_down)


def _two_source(i, na, a_ref, b_ref):
    return jnp.where(i < na, a_ref[...], b_ref[...])


def _router_kernel(xa_ref, xb_ref, g_ref, wr_ref, eid_ref, rank_ref, gate_ref, cnt_ref,
                   carry_ref, *, na):
    i = pl.program_id(0)
    tr = xa_ref.shape[0]
    n_exp = wr_ref.shape[1]

    @pl.when(i == 0)
    def _():
        carry_ref[...] = jnp.zeros_like(carry_ref)

    h = _rms(_two_source(i, na, xa_ref, xb_ref), g_ref[...])
    logits = jnp.dot(h, wr_ref[...], preferred_element_type=F32, precision=lax.Precision.HIGHEST)
    idx = lax.broadcasted_iota(I32, (tr, n_exp), 1).astype(F32)
    m1 = jnp.max(logits, axis=-1, keepdims=True)
    i1 = jnp.min(jnp.where(logits == m1, idx, float(n_exp)), axis=-1, keepdims=True)
    rest = jnp.where(idx == i1, -jnp.inf, logits)
    m2 = jnp.max(rest, axis=-1, keepdims=True)
    i2 = jnp.min(jnp.where(rest == m2, idx, float(n_exp)), axis=-1, keepdims=True)
    e2 = jnp.exp(m2 - m1)
    den = 1.0 + e2
    oh1 = (idx == i1).astype(F32)
    oh2 = (idx == i2).astype(F32)
    both = oh1 + oh2
    r = lax.broadcasted_iota(I32, (tr, tr), 0)
    c = lax.broadcasted_iota(I32, (tr, tr), 1)
    tri = jnp.where(c < r, 1.0, 0.0).astype(BF16)
    before = jnp.dot(tri, both.astype(BF16), preferred_element_type=F32) + carry_ref[...]
    rank1 = jnp.sum(oh1 * before, axis=-1, keepdims=True)
    rank2 = jnp.sum(oh2 * before, axis=-1, keepdims=True)
    carry_ref[...] += jnp.sum(both, axis=0, keepdims=True)
    k = lax.broadcasted_iota(I32, (tr, 2), 1)
    eid_ref[...] = jnp.where(k == 0, i1, i2).astype(I32)
    rank_ref[...] = jnp.where(k == 0, rank1, rank2).astype(I32)
    gate_ref[...] = jnp.where(k == 0, 1.0 / den, e2 / den)
    cnt_ref[...] = carry_ref[...].astype(I32)


def _router(xa, xb, g, w_router, tr):
    d = xa.shape[1]
    na, nb = xa.shape[0] // tr, xb.shape[0] // tr
    n_exp = w_router.shape[1]
    rows = xa.shape[0] + xb.shape[0]
    amap = lambda i: (jnp.minimum(i, na - 1), 0)
    bmap = lambda i: (jnp.maximum(i - na, 0), 0)
    pair = lambda dt: jax.ShapeDtypeStruct((rows, 2), dt)
    return pl.pallas_call(
        functools.partial(_router_kernel, na=na),
        grid=(na + nb,),
        in_specs=[pl.BlockSpec((tr, d), amap), pl.BlockSpec((tr, d), bmap),
                  pl.BlockSpec((1, d), lambda i: (0, 0)),
                  pl.BlockSpec((d, n_exp), lambda i: (0, 0))],
        out_specs=[pl.BlockSpec((tr, 2), lambda i: (i, 0))] * 3
                  + [pl.BlockSpec((1, n_exp), lambda i: (0, 0))],
        out_shape=[pair(I32), pair(I32), pair(F32), jax.ShapeDtypeStruct((1, n_exp), I32)],
        scratch_shapes=[pltpu.VMEM((1, n_exp), F32)],
        compiler_params=_params(("arbitrary",)),
        name="moe_router",
    )(xa, xb, g.reshape(1, d), w_router)


DMA_UNROLL = 8


def _dispatch_kernel(fill_ref, pos_ref, xa_ref, xb_ref, o_ref, zero_ref, sem, zsem, *, na, td, n_exp):
    i = pl.program_id(0)
    tm = zero_ref.shape[0]

    def run(x_ref):
        def copy(row, dst):
            return pltpu.make_async_copy(x_ref.at[pl.ds(row, 1)], o_ref.at[pl.ds(dst, 1)], sem)

        def issue(g, carry):
            for u in range(DMA_UNROLL):
                r = g * DMA_UNROLL + u
                copy(r, pos_ref[0, 0, 2 * r]).start()
                copy(r, pos_ref[0, 0, 2 * r + 1]).start()
            return carry
        lax.fori_loop(0, td // DMA_UNROLL, issue, 0)

        def drain(g, carry):
            for _ in range(2 * DMA_UNROLL):
                copy(0, 0).wait()
            return carry
        lax.fori_loop(0, td // DMA_UNROLL, drain, 0)

        @pl.when(i == pl.num_programs(0) - 1)
        def _():
            for e in range(n_exp):
                lo, hi = fill_ref[2 * e], fill_ref[2 * e + 1]

                def fill(r, carry):
                    copy(0, r).start()
                    return carry
                lax.fori_loop(lo, hi, fill, 0)

                def drain_one(r, carry):
                    copy(0, 0).wait()
                    return carry
                lax.fori_loop(lo, hi, drain_one, 0)

    @pl.when(i < na)
    def _():
        run(xa_ref)

    @pl.when(i >= na)
    def _():
        run(xb_ref)

    @pl.when(i == pl.num_programs(0) - 1)
    def _():
        zero_ref[...] = jnp.zeros_like(zero_ref)

        def zero_tile(t, carry):
            cp = pltpu.make_async_copy(zero_ref, o_ref.at[pl.ds(pl.multiple_of(t * tm, tm), tm)], zsem)
            cp.start()
            cp.wait()
            return carry
        lax.fori_loop(fill_ref[2 * n_exp], o_ref.shape[0] // tm, zero_tile, 0)


def _dispatch(xa, xb, pos, fill, rows_out, td, tm, n_exp):
    d = xa.shape[1]
    na, nb = xa.shape[0] // td, xb.shape[0] // td
    grid_spec = pltpu.PrefetchScalarGridSpec(
        num_scalar_prefetch=1,
        grid=(na + nb,),
        in_specs=[pl.BlockSpec((1, 1, 2 * td), lambda i, f: (i, 0, 0), memory_space=pltpu.SMEM),
                  pl.BlockSpec((td, d), lambda i, f: (jnp.minimum(i, na - 1), 0)),
                  pl.BlockSpec((td, d), lambda i, f: (jnp.maximum(i - na, 0), 0))],
        out_specs=pl.BlockSpec(memory_space=pl.ANY),
        scratch_shapes=[pltpu.VMEM((tm, d), F32), pltpu.SemaphoreType.DMA, pltpu.SemaphoreType.DMA],
    )
    return pl.pallas_call(
        functools.partial(_dispatch_kernel, na=na, td=td, n_exp=n_exp),
        grid_spec=grid_spec,
        out_shape=jax.ShapeDtypeStruct((rows_out, d), F32),
        compiler_params=_params(("arbitrary",)),
        name="moe_dispatch",
    )(fill, pos.reshape(na + nb, 1, 2 * td), xa, xb)


def _combine_kernel(pos_ref, posn_ref, xa_ref, xb_ref, gate_ref, g_ref, y_ref, oa_ref, ob_ref,
                    buf_ref, sem, *, na):
    i = pl.program_id(0)
    tc = xa_ref.shape[0]
    slot = lax.rem(i, 2)

    def copy(s, k, r, src):
        return pltpu.make_async_copy(y_ref.at[pl.ds(src, 1)], buf_ref.at[s, k, pl.ds(r, 1)],
                                     sem.at[s])

    def issue(p_ref, s):
        def body(g, carry):
            for u in range(DMA_UNROLL):
                r = g * DMA_UNROLL + u
                copy(s, 0, r, p_ref[0, 0, 2 * r]).start()
                copy(s, 1, r, p_ref[0, 0, 2 * r + 1]).start()
            return carry
        lax.fori_loop(0, tc // DMA_UNROLL, body, 0)

    @pl.when(i == 0)
    def _():
        issue(pos_ref, 0)

    @pl.when(i + 1 < pl.num_programs(0))
    def _():
        issue(posn_ref, 1 - slot)

    def drain(g, carry):
        for _ in range(2 * DMA_UNROLL):
            copy(slot, 0, 0, 0).wait()
        return carry
    lax.fori_loop(0, tc // DMA_UNROLL, drain, 0)

    x = _two_source(i, na, xa_ref, xb_ref)
    gate = gate_ref[...]
    out = _rms(x + (gate[:, 0:1] * buf_ref[slot, 0] + gate[:, 1:2] * buf_ref[slot, 1]), g_ref[...])

    @pl.when(i < na)
    def _():
        oa_ref[...] = out

    @pl.when(i >= na)
    def _():
        ob_ref[...] = out


def _combine(xa, xb, gates, pos, y_sorted, g_final, tc):
    d = xa.shape[1]
    na, nb = xa.shape[0] // tc, xb.shape[0] // tc
    n = na + nb
    amap = lambda i: (jnp.minimum(i, na - 1), 0)
    bmap = lambda i: (jnp.maximum(i - na, 0), 0)
    pos = pos.reshape(n, 1, 2 * tc)
    return pl.pallas_call(
        functools.partial(_combine_kernel, na=na),
        grid=(n,),
        in_specs=[pl.BlockSpec((1, 1, 2 * tc), lambda i: (i, 0, 0), memory_space=pltpu.SMEM),
                  pl.BlockSpec((1, 1, 2 * tc), lambda i: (jnp.minimum(i + 1, n - 1), 0, 0),
                               memory_space=pltpu.SMEM),
                  pl.BlockSpec((tc, d), amap), pl.BlockSpec((tc, d), bmap),
                  pl.BlockSpec((tc, 2), lambda i: (i, 0)),
                  pl.BlockSpec((1, d), lambda i: (0, 0)),
                  pl.BlockSpec(memory_space=pl.ANY)],
        out_specs=[pl.BlockSpec((tc, d), amap), pl.BlockSpec((tc, d), bmap)],
        out_shape=[jax.ShapeDtypeStruct(xa.shape, F32), jax.ShapeDtypeStruct(xb.shape, F32)],
        scratch_shapes=[pltpu.VMEM((2, 2, tc, d), F32), pltpu.SemaphoreType.DMA((2,))],
        compiler_params=_params(("arbitrary",)),
        name="moe_combine",
    )(pos, pos, xa, xb, gates, g_final.reshape(1, d), y_sorted)


def _moe(xa, xb, g_ffn, w_router, w_gu, w_down, g_final, tm, tf):
    n_exp = w_router.shape[1]
    rows = xa.shape[0] + xb.shape[0]
    eid, rank, gates, counts = _router(xa, xb, g_ffn, w_router, tr=256)
    counts = counts[0]
    padded = (counts + tm - 1) // tm * tm
    ends = jnp.cumsum(padded)
    offs = ends - padded
    pos = (offs[eid] + rank).reshape(-1)
    n_tiles = (2 * rows + n_exp * (tm - 1)) // tm
    n_used = (ends[-1] // tm).astype(I32)
    tile_start = jnp.arange(n_tiles, dtype=I32) * tm
    tile_expert = jnp.sum(tile_start[:, None] >= ends[None, :], axis=1).astype(I32)
    last = jnp.take(tile_expert, jnp.maximum(n_used - 1, 0))
    tile_expert = jnp.minimum(jnp.where(tile_start < ends[-1], tile_expert, last), n_exp - 1)
    fill = jnp.concatenate([jnp.stack([offs + counts, offs + padded], axis=1).reshape(-1),
                            n_used.reshape(1)]).astype(I32)
    x_sorted = _dispatch(xa, xb, pos, fill, n_tiles * tm, td=512, tm=tm, n_exp=n_exp)
    y_sorted = _ffn(x_sorted, g_ffn, w_gu, w_down, tile_expert, n_used.reshape(1), tm, tf,
                    residual=False)
    return _combine(xa, xb, gates, pos, y_sorted, g_final, tc=256)


def kernel(x_prompt, x_sample, mem_prompt, cache_swa_k, cache_swa_v, state_ssm_re, state_ssm_im, cache_mem_k, cache_mem_v, g_mix, g_ffn, g_mem, g_final, w_mem_kv, w_in_a, sinks, w_out_a, w_in_b, lam_re, lam_im, log_step, b_re, b_im, c_re, c_im, d_skip, w_glu, b_glu, w_out_b, w_ffn_gu, w_ffn_down, w_router, w_exp_gu, w_exp_down):
    n_batch, seq, d = x_prompt.shape
    n_dec, t_new, _ = x_sample.shape
    n_mem = mem_prompt.shape[1]
    depth = g_mix.shape[0]
    assert depth == 2 and w_in_a.shape[0] == 1 and w_in_b.shape[0] == 1
    n_kv = cache_swa_k.shape[3]
    n_q = sinks.shape[1]
    group = n_q // n_kv
    n_x = cache_mem_k.shape[3]
    x_q = n_x * HEAD_DIM
    kv_a = n_kv * HEAD_DIM
    q_a = n_q * HEAD_DIM
    s5_width = w_glu.shape[1]
    n_groups, n_state_g = lam_re.shape[1:]
    n_state = n_groups * n_state_g
    cache_w = cache_swa_k.shape[2]

    xp = x_prompt.reshape(n_batch * seq, d)
    xs = x_sample.reshape(n_dec * t_new, d)
    mem = mem_prompt.reshape(n_batch * n_mem, d)
    bt = 16
    chunk = 1024
    sub_len = chunk // SUBSEQ

    memkv = [_proj(mem, g_mem[i], w_mem_kv[i], tm=n_mem) for i in range(depth)]
    mem_k_prompt = jnp.stack([m[:, :x_q].reshape(n_batch, n_mem, n_x, HEAD_DIM) for m in memkv])
    mem_v_prompt = jnp.stack([m[:, x_q:].reshape(n_batch, n_mem, n_x, HEAD_DIM) for m in memkv])

    w_in = jnp.concatenate(
        [w_in_a[0][:, :q_a].reshape(d, n_kv, group, HEAD_DIM).transpose(0, 2, 1, 3).reshape(d, q_a),
         w_in_a[0][:, q_a:]], axis=1)
    w_out = jnp.concatenate(
        [w_out_a[0][:q_a].reshape(n_kv, group, HEAD_DIM, d).transpose(1, 0, 2, 3).reshape(q_a, d),
         w_out_a[0][q_a:]], axis=0)
    sink = sinks[0].reshape(n_kv, group).T.reshape(-1)
    pa_p = _proj(xp, g_mix[0], w_in, tm=512)
    pa_s = _proj(xs, g_mix[0], w_in, tm=512)
    x1p = _mixa_prompt(pa_p, xp, memkv[0], sink, w_out, n_batch, n_kv, group, n_x, n_mem)
    x1s, swa_ks, swa_vs = _mixa_sample(
        pa_s, xs, cache_swa_k[0].reshape(n_dec, cache_w, kv_a), cache_swa_v[0].reshape(n_dec, cache_w, kv_a),
        cache_mem_k[0].reshape(n_dec, n_mem, x_q), cache_mem_v[0].reshape(n_dec, n_mem, x_q),
        sink, w_out, n_kv, group, n_x, t_new, bt)
    w_keep = min(WINDOW, seq)
    kv_p = pa_p.reshape(n_batch, seq, -1)[:, seq - w_keep:, q_a:q_a + 2 * kv_a]
    swa_k_prompt = kv_p[..., :kv_a].reshape(1, n_batch, w_keep, n_kv, HEAD_DIM)
    swa_v_prompt = kv_p[..., kv_a:].reshape(1, n_batch, w_keep, n_kv, HEAD_DIM)
    swa_k_sample = swa_ks.reshape(1, n_dec, cache_w, n_kv, HEAD_DIM)
    swa_v_sample = swa_vs.reshape(1, n_dec, cache_w, n_kv, HEAD_DIM)

    x2p = _ffn(x1p, g_ffn[0], w_ffn_gu, w_ffn_down, jnp.zeros((x1p.shape[0] // 1024,), I32),
               jnp.full((1,), x1p.shape[0] // 1024, I32), 1024, 512, residual=True)
    x2s = _ffn(x1s, g_ffn[0], w_ffn_gu, w_ffn_down, jnp.zeros((x1s.shape[0] // 1024,), I32),
               jnp.full((1,), x1s.shape[0] // 1024, I32), 1024, 512, residual=True)

    pb_p = _proj(x2p, g_mix[1], w_in_b[0], tm=512)
    pb_s = _proj(x2s, g_mix[1], w_in_b[0], tm=512)
    sw = _s5_weights(lam_re[0], lam_im[0], log_step[0], b_re[0], b_im[0], c_re[0], c_im[0], sub_len)
    nchunk = seq // chunk
    u_perm = (pb_p[:, :s5_width].astype(BF16)
              .reshape(n_batch, nchunk, SUBSEQ, sub_len, s5_width)
              .transpose(0, 1, 3, 2, 4).reshape(n_batch * seq, s5_width))
    so_perm, h_fin = _s5_prompt(u_perm, sw, d_skip[0], w_glu[0], b_glu[0], n_batch, chunk)
    so_p = (so_perm.reshape(n_batch, nchunk, sub_len, SUBSEQ, s5_width)
            .transpose(0, 1, 3, 2, 4).reshape(n_batch * seq, s5_width))
    ssm_re_prompt = h_fin[:, 0, :n_state].reshape(1, n_batch, n_groups, n_state_g)
    ssm_im_prompt = h_fin[:, 0, n_state:].reshape(1, n_batch, n_groups, n_state_g)

    u_tm = (pb_s[:, :s5_width].astype(BF16).reshape(n_dec, t_new, s5_width)
            .transpose(1, 0, 2).reshape(n_dec * t_new, s5_width))
    so_tm, hr_s, hi_s = _s5_sample(u_tm, state_ssm_re[0].reshape(n_dec, n_state),
                                   state_ssm_im[0].reshape(n_dec, n_state), sw,
                                   d_skip[0], w_glu[0], b_glu[0], t_new)
    so_s = so_tm.reshape(t_new, n_dec, s5_width).transpose(1, 0, 2).reshape(n_dec * t_new, s5_width)
    ssm_re_sample = hr_s.reshape(1, n_dec, n_groups, n_state_g)
    ssm_im_sample = hi_s.reshape(1, n_dec, n_groups, n_state_g)

    x3p = _tailb_prompt(so_p, pb_p, x2p, memkv[1], w_out_b[0], n_batch, n_x, n_mem, tm=256)
    x3s = _tailb_sample(so_s, pb_s, x2s, cache_mem_k[1].reshape(n_dec, n_mem, x_q),
                        cache_mem_v[1].reshape(n_dec, n_mem, x_q), w_out_b[0], n_x, t_new, bt)

    yp, ys = _moe(x3p, x3s, g_ffn[1], w_router[0], w_exp_gu[0], w_exp_down[0], g_final, 1024, 512)
    return (yp.reshape(n_batch, seq, d), ys.reshape(n_dec, t_new, d),
            swa_k_prompt, swa_v_prompt, ssm_re_prompt, ssm_im_prompt,
            mem_k_prompt, mem_v_prompt, swa_k_sample, swa_v_sample,
            ssm_re_sample, ssm_im_sample)
```

```python
import functools

import jax
import jax.numpy as jnp
from jax import lax
from jax.experimental import pallas as pl
from jax.experimental.pallas import tpu as pltpu

F32 = jnp.float32
BF16 = jnp.bfloat16
I32 = jnp.int32

HEAD_DIM = 64
WINDOW = 128
BLOCK = 128
EPS = 1e-5
SCALE = HEAD_DIM ** -0.5
S5_GROUP = 16
SUBSEQ = 8
SUBLANES = 8
LANE_WIN = 512
VMEM_LIMIT = 56 * 1024 * 1024

_NT = (((1,), (1,)), ((), ()))


def _params(sem, vmem=VMEM_LIMIT):
    return pltpu.CompilerParams(dimension_semantics=sem, vmem_limit_bytes=vmem)


def _rms(x, g):
    return x * lax.rsqrt(jnp.mean(x * x, axis=-1, keepdims=True) + EPS) * g


def _softmax(s):
    m = jnp.max(s, axis=-1, keepdims=True)
    p = jnp.exp(s - m)
    return p / jnp.sum(p, axis=-1, keepdims=True)


def _proj_kernel(x_ref, g_ref, w_ref, o_ref):
    h = _rms(x_ref[...], g_ref[...]).astype(BF16)
    o_ref[...] = jnp.dot(h, w_ref[...], preferred_element_type=F32)


def _proj(x, g, w, tm):
    rows, d = x.shape
    n = w.shape[1]
    return pl.pallas_call(
        _proj_kernel,
        grid=(rows // tm,),
        in_specs=[pl.BlockSpec((tm, d), lambda i: (i, 0)),
                  pl.BlockSpec((1, d), lambda i: (0, 0)),
                  pl.BlockSpec((d, n), lambda i: (0, 0))],
        out_specs=pl.BlockSpec((tm, n), lambda i: (i, 0)),
        out_shape=jax.ShapeDtypeStruct((rows, n), F32),
        compiler_params=_params(("parallel",)),
        name="rms_proj",
    )(x, g.reshape(1, d), w.astype(BF16))


def _sink_attention_group(q_rows, k_h, v_h, mask, sink_col):
    s = lax.dot_general(q_rows, k_h, _NT, preferred_element_type=F32)
    s = jnp.where(mask, s, -jnp.inf)
    m = jnp.maximum(jnp.max(s, axis=-1, keepdims=True), sink_col)
    p = jnp.exp(s - m)
    den = jnp.sum(p, axis=-1, keepdims=True) + jnp.exp(sink_col - m)
    return jnp.dot((p / den).astype(BF16), v_h, preferred_element_type=F32)


def _cross_attention(xq, mk, mv, n_heads, mix_ref, rows, col0):
    xq = (xq * SCALE).astype(BF16)
    for h in range(n_heads):
        sl = slice(h * HEAD_DIM, (h + 1) * HEAD_DIM)
        s = lax.dot_general(xq[:, sl], mk[:, sl], _NT, preferred_element_type=F32)
        o = jnp.dot(_softmax(s).astype(BF16), mv[:, sl], preferred_element_type=F32)
        mix_ref[rows, col0 + h * HEAD_DIM:col0 + (h + 1) * HEAD_DIM] = o


def _sink_column(sink_ref, heads, rows_per_head):
    n = len(heads)
    row = lax.broadcasted_iota(I32, (n * rows_per_head, 1), 0)
    col = jnp.full((n * rows_per_head, 1), sink_ref[0, heads[n - 1]], F32)
    for g in range(n - 2, -1, -1):
        col = jnp.where(row < (g + 1) * rows_per_head, sink_ref[0, heads[g]], col)
    return col


def _head_mask(n_heads, t):
    shape = (n_heads * t, n_heads * HEAD_DIM)
    row = lax.broadcasted_iota(I32, shape, 0)
    col = lax.broadcasted_iota(I32, shape, 1)
    lo = (row // t) * HEAD_DIM
    return (col >= lo) & (col < lo + HEAD_DIM)


def _spread_heads(x, hmask, n_heads):
    return jnp.where(hmask, jnp.concatenate([x] * n_heads, axis=0), 0.0).astype(BF16)


def _gather_heads(o, hmask, n_heads, t):
    o = jnp.where(hmask, o, 0.0)
    acc = o[0:t]
    for h in range(1, n_heads):
        acc = acc + o[h * t:(h + 1) * t]
    return acc


def _cross_attention_small(xq, mk, mv, hmask, n_heads):
    t = xq.shape[0]
    s = lax.dot_general(_spread_heads(xq * SCALE, hmask, n_heads), mk, _NT,
                        preferred_element_type=F32)
    o = jnp.dot(_softmax(s).astype(BF16), mv, preferred_element_type=F32)
    return _gather_heads(o, hmask, n_heads, t)


def _out_proj(x_ref, mix_ref, wo_ref, o_ref):
    o_ref[...] = x_ref[...] + jnp.dot(mix_ref[...].astype(BF16), wo_ref[...],
                                      preferred_element_type=F32)


def _mixa_prompt_kernel(sink_ref, q_ref, kp_ref, kc_ref, vp_ref, vc_ref, xq_ref, x_ref,
                        mk_ref, mv_ref, wo_ref, o_ref, mix_ref, *, n_kv, group, n_x):
    i = pl.program_id(1)
    q = q_ref[...] * SCALE
    kk = jnp.concatenate([kp_ref[...], kc_ref[...]], axis=0).astype(BF16)
    vv = jnp.concatenate([vp_ref[...], vc_ref[...]], axis=0).astype(BF16)
    m_rows = group * BLOCK
    qi = lax.rem(lax.broadcasted_iota(I32, (m_rows, 2 * BLOCK), 0), BLOCK)
    kj = lax.broadcasted_iota(I32, (m_rows, 2 * BLOCK), 1)
    dist = qi + BLOCK - kj
    first = jnp.where(i > 0, 0, BLOCK)
    mask = (dist >= 0) & (dist <= WINDOW) & (kj >= first)
    for h in range(n_kv):
        sl = slice(h * HEAD_DIM, (h + 1) * HEAD_DIM)
        cols = [(g * n_kv + h) * HEAD_DIM for g in range(group)]
        qh = jnp.concatenate([q[:, c0:c0 + HEAD_DIM] for c0 in cols], axis=0).astype(BF16)
        o = _sink_attention_group(qh, kk[:, sl], vv[:, sl], mask,
                                  _sink_column(sink_ref, [g * n_kv + h for g in range(group)], BLOCK))
        for g, c0 in enumerate(cols):
            mix_ref[:, c0:c0 + HEAD_DIM] = o[g * BLOCK:(g + 1) * BLOCK]
    q_a = n_kv * group * HEAD_DIM
    _cross_attention(xq_ref[...], mk_ref[...].astype(BF16), mv_ref[...].astype(BF16), n_x,
                     mix_ref, slice(None), q_a)
    _out_proj(x_ref, mix_ref, wo_ref, o_ref)


def _mixa_prompt(p, x, memkv, sinks, w_out, n_batch, n_kv, group, n_x, n_mem):
    rows, d = x.shape
    nb = rows // n_batch // BLOCK
    q_a = n_kv * group * HEAD_DIM
    kv_a = n_kv * HEAD_DIM
    x_q = n_x * HEAD_DIM
    kcol = q_a // kv_a
    vcol = (q_a + kv_a) // kv_a
    xcol = (q_a + 2 * kv_a) // x_q
    cur = lambda b, i: b * nb + i
    prev = lambda b, i: b * nb + jnp.maximum(i - 1, 0)
    kern = functools.partial(_mixa_prompt_kernel, n_kv=n_kv, group=group, n_x=n_x)
    return pl.pallas_call(
        kern,
        grid=(n_batch, nb),
        in_specs=[
            pl.BlockSpec(memory_space=pltpu.SMEM),
            pl.BlockSpec((BLOCK, q_a), lambda b, i: (cur(b, i), 0)),
            pl.BlockSpec((BLOCK, kv_a), lambda b, i: (prev(b, i), kcol)),
            pl.BlockSpec((BLOCK, kv_a), lambda b, i: (cur(b, i), kcol)),
            pl.BlockSpec((BLOCK, kv_a), lambda b, i: (prev(b, i), vcol)),
            pl.BlockSpec((BLOCK, kv_a), lambda b, i: (cur(b, i), vcol)),
            pl.BlockSpec((BLOCK, x_q), lambda b, i: (cur(b, i), xcol)),
            pl.BlockSpec((BLOCK, d), lambda b, i: (cur(b, i), 0)),
            pl.BlockSpec((n_mem, x_q), lambda b, i: (b, 0)),
            pl.BlockSpec((n_mem, x_q), lambda b, i: (b, 1)),
            pl.BlockSpec((q_a + x_q, d), lambda b, i: (0, 0)),
        ],
        out_specs=pl.BlockSpec((BLOCK, d), lambda b, i: (cur(b, i), 0)),
        out_shape=jax.ShapeDtypeStruct((rows, d), F32),
        scratch_shapes=[pltpu.VMEM((BLOCK, q_a + x_q), F32)],
        compiler_params=_params(("parallel", "parallel")),
        name="mix_a_prompt",
    )(sinks.reshape(1, -1), p, p, p, p, p, p, x, memkv, memkv, w_out.astype(BF16))


def _mixa_sample_kernel(sink_ref, p_ref, x_ref, ck_ref, cv_ref, mk_ref, mv_ref, wo_ref,
                        o_ref, ok_ref, ov_ref, mix_ref, *, n_kv, group, n_x, t_new, bt):
    w = ck_ref.shape[1]
    q_a = n_kv * group * HEAD_DIM
    kv_a = n_kv * HEAD_DIM
    m_rows = group * n_kv * t_new
    ti = lax.rem(lax.broadcasted_iota(I32, (m_rows, w + t_new), 0), t_new)
    kj = lax.broadcasted_iota(I32, (m_rows, w + t_new), 1)
    dist = ti + w - kj
    mask = (dist >= 0) & (dist <= WINDOW)
    kv_mask = _head_mask(n_kv, t_new)
    x_mask = _head_mask(n_x, t_new)
    sink_col = sink_ref[...]

    def body(bb, carry):
        rows = pl.ds(pl.multiple_of(bb * t_new, t_new), t_new)
        q = p_ref[rows, 0:q_a] * SCALE
        k_new = p_ref[rows, q_a:q_a + kv_a]
        v_new = p_ref[rows, q_a + kv_a:q_a + 2 * kv_a]
        xq = p_ref[rows, q_a + 2 * kv_a:]
        k_old = ck_ref[bb]
        v_old = cv_ref[bb]
        ok_ref[bb, 0:w - t_new, :] = k_old[t_new:, :]
        ok_ref[bb, w - t_new:w, :] = k_new
        ov_ref[bb, 0:w - t_new, :] = v_old[t_new:, :]
        ov_ref[bb, w - t_new:w, :] = v_new
        kk = jnp.concatenate([k_old, k_new], axis=0).astype(BF16)
        vv = jnp.concatenate([v_old, v_new], axis=0).astype(BF16)
        qs = jnp.concatenate([_spread_heads(q[:, g * kv_a:(g + 1) * kv_a], kv_mask, n_kv)
                              for g in range(group)], axis=0)
        o = _sink_attention_group(qs, kk, vv, mask, sink_col)
        blk = n_kv * t_new
        for g in range(group):
            mix_ref[rows, g * kv_a:(g + 1) * kv_a] = _gather_heads(
                o[g * blk:(g + 1) * blk], kv_mask, n_kv, t_new)
        mix_ref[rows, q_a:] = _cross_attention_small(
            xq, mk_ref[bb].astype(BF16), mv_ref[bb].astype(BF16), x_mask, n_x)
        return carry

    lax.fori_loop(0, bt, body, 0, unroll=2)
    _out_proj(x_ref, mix_ref, wo_ref, o_ref)


def _mixa_sample(p, x, cache_k, cache_v, mem_k, mem_v, sinks, w_out, n_kv, group, n_x, t_new, bt):
    rows, d = x.shape
    n_dec, w, kv_a = cache_k.shape
    n_mem, x_q = mem_k.shape[1:]
    q_a = n_kv * group * HEAD_DIM
    tm = bt * t_new
    kern = functools.partial(_mixa_sample_kernel, n_kv=n_kv, group=group, n_x=n_x,
                             t_new=t_new, bt=bt)
    return pl.pallas_call(
        kern,
        grid=(n_dec // bt,),
        in_specs=[
            pl.BlockSpec((n_kv * group * t_new, 1), lambda i: (0, 0)),
            pl.BlockSpec((tm, p.shape[1]), lambda i: (i, 0)),
            pl.BlockSpec((tm, d), lambda i: (i, 0)),
            pl.BlockSpec((bt, w, kv_a), lambda i: (i, 0, 0)),
            pl.BlockSpec((bt, w, kv_a), lambda i: (i, 0, 0)),
            pl.BlockSpec((bt, n_mem, x_q), lambda i: (i, 0, 0)),
            pl.BlockSpec((bt, n_mem, x_q), lambda i: (i, 0, 0)),
            pl.BlockSpec((q_a + x_q, d), lambda i: (0, 0)),
        ],
        out_specs=[pl.BlockSpec((tm, d), lambda i: (i, 0)),
                   pl.BlockSpec((bt, w, kv_a), lambda i: (i, 0, 0)),
                   pl.BlockSpec((bt, w, kv_a), lambda i: (i, 0, 0))],
        out_shape=[jax.ShapeDtypeStruct((rows, d), F32),
                   jax.ShapeDtypeStruct((n_dec, w, kv_a), F32),
                   jax.ShapeDtypeStruct((n_dec, w, kv_a), F32)],
        scratch_shapes=[pltpu.VMEM((tm, q_a + x_q), F32)],
        compiler_params=_params(("parallel",)),
        name="mix_a_sample",
    )(jnp.repeat(sinks, t_new).reshape(-1, 1), p, x, cache_k, cache_v, mem_k, mem_v,
      w_out.astype(BF16))


def _tailb_prompt_kernel(s_ref, xq_ref, x_ref, mk_ref, mv_ref, wo_ref, o_ref, mix_ref, *, n_x):
    width = s_ref.shape[1]
    mix_ref[:, 0:width] = s_ref[...].astype(F32)
    _cross_attention(xq_ref[...], mk_ref[...].astype(BF16), mv_ref[...].astype(BF16), n_x,
                     mix_ref, slice(None), width)
    _out_proj(x_ref, mix_ref, wo_ref, o_ref)


def _tailb_prompt(s_out, p, x, memkv, w_out, n_batch, n_x, n_mem, tm):
    rows, d = x.shape
    width = s_out.shape[1]
    x_q = n_x * HEAD_DIM
    per_b = rows // n_batch // tm
    kern = functools.partial(_tailb_prompt_kernel, n_x=n_x)
    return pl.pallas_call(
        kern,
        grid=(rows // tm,),
        in_specs=[
            pl.BlockSpec((tm, width), lambda i: (i, 0)),
            pl.BlockSpec((tm, x_q), lambda i: (i, width // x_q)),
            pl.BlockSpec((tm, d), lambda i: (i, 0)),
            pl.BlockSpec((n_mem, x_q), lambda i: (i // per_b, 0)),
            pl.BlockSpec((n_mem, x_q), lambda i: (i // per_b, 1)),
            pl.BlockSpec((width + x_q, d), lambda i: (0, 0)),
        ],
        out_specs=pl.BlockSpec((tm, d), lambda i: (i, 0)),
        out_shape=jax.ShapeDtypeStruct((rows, d), F32),
        scratch_shapes=[pltpu.VMEM((tm, width + x_q), F32)],
        compiler_params=_params(("parallel",)),
        name="tail_b_prompt",
    )(s_out, p, x, memkv, memkv, w_out.astype(BF16))


def _tailb_sample_kernel(s_ref, xq_ref, x_ref, mk_ref, mv_ref, wo_ref, o_ref, mix_ref,
                         *, n_x, t_new, bt):
    width = s_ref.shape[1]
    mix_ref[:, 0:width] = s_ref[...].astype(F32)

    x_mask = _head_mask(n_x, t_new)

    def body(bb, carry):
        rows = pl.ds(pl.multiple_of(bb * t_new, t_new), t_new)
        mix_ref[rows, width:] = _cross_attention_small(
            xq_ref[rows, :], mk_ref[bb].astype(BF16), mv_ref[bb].astype(BF16), x_mask, n_x)
        return carry

    lax.fori_loop(0, bt, body, 0, unroll=2)
    _out_proj(x_ref, mix_ref, wo_ref, o_ref)


def _tailb_sample(s_out, p, x, mem_k, mem_v, w_out, n_x, t_new, bt):
    rows, d = x.shape
    width = s_out.shape[1]
    n_dec, n_mem, x_q = mem_k.shape
    tm = bt * t_new
    kern = functools.partial(_tailb_sample_kernel, n_x=n_x, t_new=t_new, bt=bt)
    return pl.pallas_call(
        kern,
        grid=(n_dec // bt,),
        in_specs=[
            pl.BlockSpec((tm, width), lambda i: (i, 0)),
            pl.BlockSpec((tm, x_q), lambda i: (i, width // x_q)),
            pl.BlockSpec((tm, d), lambda i: (i, 0)),
            pl.BlockSpec((bt, n_mem, x_q), lambda i: (i, 0, 0)),
            pl.BlockSpec((bt, n_mem, x_q), lambda i: (i, 0, 0)),
            pl.BlockSpec((width + x_q, d), lambda i: (0, 0)),
        ],
        out_specs=pl.BlockSpec((tm, d), lambda i: (i, 0)),
        out_shape=jax.ShapeDtypeStruct((rows, d), F32),
        scratch_shapes=[pltpu.VMEM((tm, width + x_q), F32)],
        compiler_params=_params(("parallel",)),
        name="tail_b_sample",
    )(s_out, p, x, mem_k, mem_v, w_out.astype(BF16))


def _s5_disc_kernel(lr_ref, li_ref, ls_ref, br_ref, bi_ref,
                    ar_ref, ai_ref, apr_ref, api_ref, bbr_ref, bbi_ref, *, n_square):
    lr, li = lr_ref[...], li_ref[...]
    step = jnp.exp(ls_ref[...])
    mag = jnp.exp(lr * step)
    ar, ai = mag * jnp.cos(li * step), mag * jnp.sin(li * step)
    den = lr * lr + li * li
    cr = ((ar - 1.0) * lr + ai * li) / den
    ci = (ai * lr - (ar - 1.0) * li) / den
    br, bi = br_ref[...], bi_ref[...]
    bbr_ref[...] = cr * br - ci * bi
    bbi_ref[...] = cr * bi + ci * br
    ar_ref[...] = ar
    ai_ref[...] = ai
    pr, pi = ar, ai
    for _ in range(n_square):
        pr, pi = pr * pr - pi * pi, 2.0 * pr * pi
    apr_ref[...] = pr
    api_ref[...] = pi


def _s5_weights(lam_re, lam_im, log_step, b_re, b_im, c_re, c_im, sub_len):
    g, p = lam_re.shape
    hh = b_re.shape[2]
    rep = lambda a: jnp.repeat(a, hh, axis=0)
    flat = lambda a: a.transpose(0, 2, 1).reshape(g * hh, p)
    n_square = sub_len.bit_length() - 1
    assert 1 << n_square == sub_len
    shp = jax.ShapeDtypeStruct((g * hh, p), F32)
    ar, ai, apr, api, bbr, bbi = pl.pallas_call(
        functools.partial(_s5_disc_kernel, n_square=n_square),
        out_shape=[shp] * 6,
        name="s5_discretise",
    )(rep(lam_re), rep(lam_im), rep(jnp.broadcast_to(log_step[:, None], (g, p))),
      flat(b_re), flat(b_im))
    row = lambda a: a[::hh].reshape(1, g * p)
    gb = 256 // hh
    nblk = g // gb
    eye = jnp.eye(gb, dtype=F32)
    bd_in = lambda a: (a.reshape(nblk, gb, hh, 1, p) * eye[None, :, None, :, None]
                       ).reshape(nblk, gb * hh, gb * p)
    wb = jnp.concatenate([bd_in(bbr), bd_in(bbi)], axis=-1).astype(BF16)
    bd_out = lambda c: (c.reshape(nblk, gb, hh, p).transpose(0, 1, 3, 2)[:, :, :, None, :]
                        * eye[None, :, None, :, None]).reshape(nblk, gb * p, gb * hh)
    wc = jnp.concatenate([bd_out(c_re), -bd_out(c_im)], axis=1).astype(BF16)
    return row(ar), row(ai), row(apr), row(api), wb, wc


def _cstep(ar, ai, hr, hi, xr, xi):
    return ar * hr - ai * hi + xr, ar * hi + ai * hr + xi


def _glu_out(y, u, d_ref, wg_ref, bg_ref):
    z = jax.nn.gelu(y + d_ref[...] * u)
    gate = jnp.dot(z.astype(BF16), wg_ref[...], preferred_element_type=F32) + bg_ref[...]
    return z * jax.nn.sigmoid(gate)


def _s5_prompt_kernel(u_ref, ar_ref, ai_ref, apr_ref, api_ref, wb_ref, wc_ref, d_ref, wg_ref,
                      bg_ref, o_ref, hf_ref, xs_ref, y_ref, carry_ref, *, n_steps):
    c = pl.program_id(1)
    nblk, ublk, two_cw = wb_ref.shape
    cw = two_cw // 2
    n_state = nblk * cw

    @pl.when(c == 0)
    def _():
        carry_ref[...] = jnp.zeros_like(carry_ref)

    row8 = lax.broadcasted_iota(I32, (SUBLANES, LANE_WIN), 0)
    for j in range(nblk):
        xs_ref[...] = jnp.dot(u_ref[:, j * ublk:(j + 1) * ublk], wb_ref[j],
                              preferred_element_type=F32)
        for lb in range(cw // LANE_WIN):
            re = slice(lb * LANE_WIN, (lb + 1) * LANE_WIN)
            im = slice(cw + lb * LANE_WIN, cw + (lb + 1) * LANE_WIN)
            gre = slice(j * cw + lb * LANE_WIN, j * cw + (lb + 1) * LANE_WIN)
            gim = slice(n_state + j * cw + lb * LANE_WIN, n_state + j * cw + (lb + 1) * LANE_WIN)
            bc = lambda r: jnp.broadcast_to(r[:, gre], (SUBLANES, LANE_WIN))
            ar, ai, apr, api = bc(ar_ref), bc(ai_ref), bc(apr_ref), bc(api_ref)
            c_r = jnp.broadcast_to(carry_ref[:, gre], (SUBLANES, LANE_WIN))
            c_i = jnp.broadcast_to(carry_ref[:, gim], (SUBLANES, LANE_WIN))

            def rows(i):
                return pl.ds(pl.multiple_of(i * SUBLANES, SUBLANES), SUBLANES)

            def pass1(i, h):
                return _cstep(ar, ai, h[0], h[1], xs_ref[rows(i), re], xs_ref[rows(i), im])

            zero = jnp.zeros((SUBLANES, LANE_WIN), F32)
            e_r, e_i = lax.fori_loop(0, n_steps, pass1, (zero, zero), unroll=4)
            s_r = jnp.where(row8 == 0, c_r, 0.0)
            s_i = jnp.where(row8 == 0, c_i, 0.0)
            for _ in range(SUBSEQ - 1):
                n_r, n_i = _cstep(apr, api, s_r, s_i, e_r, e_i)
                s_r = jnp.where(row8 == 0, c_r, pltpu.roll(n_r, 1, 0))
                s_i = jnp.where(row8 == 0, c_i, pltpu.roll(n_i, 1, 0))
            n_r, n_i = _cstep(apr, api, s_r, s_i, e_r, e_i)
            carry_ref[:, gre] = n_r[SUBSEQ - 1:SUBSEQ, :]
            carry_ref[:, gim] = n_i[SUBSEQ - 1:SUBSEQ, :]

            def pass2(i, h):
                h = _cstep(ar, ai, h[0], h[1], xs_ref[rows(i), re], xs_ref[rows(i), im])
                xs_ref[rows(i), re] = h[0]
                xs_ref[rows(i), im] = h[1]
                return h

            lax.fori_loop(0, n_steps, pass2, (s_r, s_i), unroll=4)
        y_ref[:, j * ublk:(j + 1) * ublk] = jnp.dot(xs_ref[...].astype(BF16), wc_ref[j],
                                                    preferred_element_type=F32)
    o_ref[...] = _glu_out(y_ref[...], u_ref[...].astype(F32), d_ref, wg_ref, bg_ref).astype(o_ref.dtype)
    hf_ref[0] = carry_ref[...]


def _s5_prompt(u_perm, sw, d_skip, w_glu, b_glu, n_batch, chunk):
    rows, width = u_perm.shape
    ar, ai, apr, api, wb, wc = sw
    nblk, ublk, two_cw = wb.shape
    n_state = ar.shape[1]
    nchunk = rows // n_batch // chunk
    const2 = lambda b, c: (0, 0)
    const3 = lambda b, c: (0, 0, 0)
    kern = functools.partial(_s5_prompt_kernel, n_steps=chunk // SUBSEQ)
    return pl.pallas_call(
        kern,
        grid=(n_batch, nchunk),
        in_specs=[
            pl.BlockSpec((chunk, width), lambda b, c: (b * nchunk + c, 0)),
            pl.BlockSpec((1, n_state), const2), pl.BlockSpec((1, n_state), const2),
            pl.BlockSpec((1, n_state), const2), pl.BlockSpec((1, n_state), const2),
            pl.BlockSpec(wb.shape, const3), pl.BlockSpec(wc.shape, const3),
            pl.BlockSpec((1, width), const2), pl.BlockSpec((width, width), const2),
            pl.BlockSpec((1, width), const2),
        ],
        out_specs=[pl.BlockSpec((chunk, width), lambda b, c: (b * nchunk + c, 0)),
                   pl.BlockSpec((1, 1, 2 * n_state), lambda b, c: (b, 0, 0))],
        out_shape=[jax.ShapeDtypeStruct((rows, width), BF16),
                   jax.ShapeDtypeStruct((n_batch, 1, 2 * n_state), F32)],
        scratch_shapes=[pltpu.VMEM((chunk, two_cw), F32), pltpu.VMEM((chunk, width), F32),
                        pltpu.VMEM((1, 2 * n_state), F32)],
        compiler_params=_params(("arbitrary", "arbitrary")),
        name="s5_prompt",
    )(u_perm, ar, ai, apr, api, wb, wc, d_skip.reshape(1, -1), w_glu.astype(BF16),
      b_glu.reshape(1, -1))


def _s5_sample_kernel(u_ref, h0r_ref, h0i_ref, ar_ref, ai_ref, wb_ref, wc_ref, d_ref, wg_ref,
                      bg_ref, o_ref, hr_ref, hi_ref, xs_ref, y_ref, *, t_new):
    nblk, ublk, two_cw = wb_ref.shape
    cw = two_cw // 2
    n_seq = h0r_ref.shape[0]
    for j in range(nblk):
        xs_ref[...] = jnp.dot(u_ref[:, j * ublk:(j + 1) * ublk], wb_ref[j],
                              preferred_element_type=F32)
        for lb in range(cw // LANE_WIN):
            re = slice(lb * LANE_WIN, (lb + 1) * LANE_WIN)
            im = slice(cw + lb * LANE_WIN, cw + (lb + 1) * LANE_WIN)
            gre = slice(j * cw + lb * LANE_WIN, j * cw + (lb + 1) * LANE_WIN)
            ar = jnp.broadcast_to(ar_ref[:, gre], (SUBLANES, LANE_WIN))
            ai = jnp.broadcast_to(ai_ref[:, gre], (SUBLANES, LANE_WIN))

            def group_body(rg, carry):
                base = pl.multiple_of(rg * SUBLANES, SUBLANES)
                h = (h0r_ref[pl.ds(base, SUBLANES), gre], h0i_ref[pl.ds(base, SUBLANES), gre])
                for t in range(t_new):
                    r = pl.ds(t * n_seq + base, SUBLANES)
                    h = _cstep(ar, ai, h[0], h[1], xs_ref[r, re], xs_ref[r, im])
                    xs_ref[r, re] = h[0]
                    xs_ref[r, im] = h[1]
                hr_ref[pl.ds(base, SUBLANES), gre] = h[0]
                hi_ref[pl.ds(base, SUBLANES), gre] = h[1]
                return carry

            lax.fori_loop(0, n_seq // SUBLANES, group_body, 0)
        y_ref[:, j * ublk:(j + 1) * ublk] = jnp.dot(xs_ref[...].astype(BF16), wc_ref[j],
                                                    preferred_element_type=F32)
    o_ref[...] = _glu_out(y_ref[...], u_ref[...].astype(F32), d_ref, wg_ref, bg_ref).astype(o_ref.dtype)


def _s5_sample(u_tm, h0r, h0i, sw, d_skip, w_glu, b_glu, t_new):
    rows, width = u_tm.shape
    ar, ai, _, _, wb, wc = sw
    two_cw = wb.shape[2]
    n_seq, n_state = h0r.shape
    kern = functools.partial(_s5_sample_kernel, t_new=t_new)
    return pl.pallas_call(
        kern,
        out_shape=[jax.ShapeDtypeStruct((rows, width), BF16),
                   jax.ShapeDtypeStruct((n_seq, n_state), F32),
                   jax.ShapeDtypeStruct((n_seq, n_state), F32)],
        scratch_shapes=[pltpu.VMEM((rows, two_cw), F32), pltpu.VMEM((rows, width), F32)],
        compiler_params=pltpu.CompilerParams(vmem_limit_bytes=VMEM_LIMIT),
        name="s5_sample",
    )(u_tm, h0r, h0i, ar, ai, wb, wc, d_skip.reshape(1, -1), w_glu.astype(BF16),
      b_glu.reshape(1, -1))


def _ffn_kernel(te_ref, nu_ref, x_ref, g_ref, wg_ref, wu_ref, wd_ref, o_ref, hn_ref, acc_ref,
                *, residual):
    i, j = pl.program_id(0), pl.program_id(1)

    @pl.when(i < nu_ref[0])
    def _():
        @pl.when(j == 0)
        def _():
            hn_ref[...] = _rms(x_ref[...], g_ref[...]).astype(BF16)
            acc_ref[...] = jnp.zeros_like(acc_ref)

        h = hn_ref[...]
        gate = jnp.dot(h, wg_ref[0].astype(BF16), preferred_element_type=F32)
        up = jnp.dot(h, wu_ref[0].astype(BF16), preferred_element_type=F32)
        act = (gate * jax.nn.sigmoid(gate) * up).astype(BF16)
        acc_ref[...] += jnp.dot(act, wd_ref[0].astype(BF16), preferred_element_type=F32)

        @pl.when(j == pl.num_programs(1) - 1)
        def _():
            o_ref[...] = (x_ref[...] + acc_ref[...]) if residual else acc_ref[...]

    @pl.when((i >= nu_ref[0]) & (j == 0))
    def _():
        o_ref[...] = jnp.zeros_like(o_ref)


def _ffn(x, g, w_gu, w_down, tile_expert, n_used, tm, tf, residual):
    rows, d = x.shape
    d_ff = w_down.shape[1]
    nf = d_ff // tf
    row_map = lambda i, j, te, nu: (jnp.maximum(jnp.minimum(i, nu[0] - 1), 0), 0)
    jj = lambda i, j, nu: jnp.where(i < nu[0], j, nf - 1)
    kern = functools.partial(_ffn_kernel, residual=residual)
    grid_spec = pltpu.PrefetchScalarGridSpec(
        num_scalar_prefetch=2,
        grid=(rows // tm, nf),
        in_specs=[
            pl.BlockSpec((tm, d), row_map),
            pl.BlockSpec((1, d), lambda i, j, te, nu: (0, 0)),
            pl.BlockSpec((1, d, tf), lambda i, j, te, nu: (te[i], 0, jj(i, j, nu))),
            pl.BlockSpec((1, d, tf), lambda i, j, te, nu: (te[i], 0, nf + jj(i, j, nu))),
            pl.BlockSpec((1, tf, d), lambda i, j, te, nu: (te[i], jj(i, j, nu), 0)),
        ],
        out_specs=pl.BlockSpec((tm, d), lambda i, j, te, nu: (i, 0)),
        scratch_shapes=[pltpu.VMEM((tm, d), BF16), pltpu.VMEM((tm, d), F32)],
    )
    return pl.pallas_call(
        kern,
        grid_spec=grid_spec,
        out_shape=jax.ShapeDtypeStruct((rows, d), F32),
        compiler_params=_params(("arbitrary", "arbitrary")),
        name="swiglu_ffn",
    )(tile_expert, n_used, x, g.reshape(1, d), w_gu, w_gu, w_down)


def _two_source(i, na, a_ref, b_ref):
    return jnp.where(i < na, a_ref[...], b_ref[...])


def _router_kernel(xa_ref, xb_ref, g_ref, wr_ref, eid_ref, rank_ref, gate_ref, cnt_ref,
                   carry_ref, *, na):
    i = pl.program_id(0)
    tr = xa_ref.shape[0]
    n_exp = wr_ref.shape[1]

    @pl.when(i == 0)
    def _():
        carry_ref[...] = jnp.zeros_like(carry_ref)

    h = _rms(_two_source(i, na, xa_ref, xb_ref), g_ref[...])
    w = wr_ref[...]
    h_hi, w_hi = h.astype(BF16), w.astype(BF16)
    h_lo = (h - h_hi.astype(F32)).astype(BF16)
    w_lo = (w - w_hi.astype(F32)).astype(BF16)
    logits = (jnp.dot(h_hi, w_hi, preferred_element_type=F32)
              + (jnp.dot(h_hi, w_lo, preferred_element_type=F32)
                 + jnp.dot(h_lo, w_hi, preferred_element_type=F32)))
    idx = lax.broadcasted_iota(I32, (tr, n_exp), 1).astype(F32)
    m1 = jnp.max(logits, axis=-1, keepdims=True)
    i1 = jnp.min(jnp.where(logits == m1, idx, float(n_exp)), axis=-1, keepdims=True)
    rest = jnp.where(idx == i1, -jnp.inf, logits)
    m2 = jnp.max(rest, axis=-1, keepdims=True)
    i2 = jnp.min(jnp.where(rest == m2, idx, float(n_exp)), axis=-1, keepdims=True)
    e2 = jnp.exp(m2 - m1)
    den = 1.0 + e2
    oh1 = (idx == i1).astype(F32)
    oh2 = (idx == i2).astype(F32)
    both = oh1 + oh2
    r = lax.broadcasted_iota(I32, (tr, tr), 0)
    c = lax.broadcasted_iota(I32, (tr, tr), 1)
    tri = jnp.where(c < r, 1.0, 0.0).astype(BF16)
    before = jnp.dot(tri, both.astype(BF16), preferred_element_type=F32) + carry_ref[...]
    rank1 = jnp.sum(oh1 * before, axis=-1, keepdims=True)
    rank2 = jnp.sum(oh2 * before, axis=-1, keepdims=True)
    carry_ref[...] += jnp.sum(both, axis=0, keepdims=True)
    k = lax.broadcasted_iota(I32, (tr, 2), 1)
    eid_ref[...] = jnp.where(k == 0, i1, i2).astype(I32)
    rank_ref[...] = jnp.where(k == 0, rank1, rank2).astype(I32)
    gate_ref[...] = jnp.where(k == 0, 1.0 / den, e2 / den)
    cnt_ref[...] = carry_ref[...].astype(I32)


def _router(xa, xb, g, w_router, tr):
    d = xa.shape[1]
    na, nb = xa.shape[0] // tr, xb.shape[0] // tr
    n_exp = w_router.shape[1]
    rows = xa.shape[0] + xb.shape[0]
    amap = lambda i: (jnp.minimum(i, na - 1), 0)
    bmap = lambda i: (jnp.maximum(i - na, 0), 0)
    pair = lambda dt: jax.ShapeDtypeStruct((rows, 2), dt)
    return pl.pallas_call(
        functools.partial(_router_kernel, na=na),
        grid=(na + nb,),
        in_specs=[pl.BlockSpec((tr, d), amap), pl.BlockSpec((tr, d), bmap),
                  pl.BlockSpec((1, d), lambda i: (0, 0)),
                  pl.BlockSpec((d, n_exp), lambda i: (0, 0))],
        out_specs=[pl.BlockSpec((tr, 2), lambda i: (i, 0))] * 3
                  + [pl.BlockSpec((1, n_exp), lambda i: (0, 0))],
        out_shape=[pair(I32), pair(I32), pair(F32), jax.ShapeDtypeStruct((1, n_exp), I32)],
        scratch_shapes=[pltpu.VMEM((1, n_exp), F32)],
        compiler_params=_params(("arbitrary",)),
        name="moe_router",
    )(xa, xb, g.reshape(1, d), w_router)


DMA_UNROLL = 8


def _dispatch_kernel(fill_ref, pos_ref, xa_ref, xb_ref, o_ref, zero_ref, sem, zsem, *, na, td, n_exp):
    i = pl.program_id(0)
    tm = zero_ref.shape[0]

    def run(x_ref):
        def copy(row, dst):
            return pltpu.make_async_copy(x_ref.at[pl.ds(row, 1)], o_ref.at[pl.ds(dst, 1)], sem)

        def issue(g, carry):
            for u in range(DMA_UNROLL):
                r = g * DMA_UNROLL + u
                copy(r, pos_ref[0, 0, 2 * r]).start(priority=0)
                copy(r, pos_ref[0, 0, 2 * r + 1]).start(priority=1)
            return carry
        lax.fori_loop(0, td // DMA_UNROLL, issue, 0)

        def drain(g, carry):
            for _ in range(2 * DMA_UNROLL):
                copy(0, 0).wait()
            return carry
        lax.fori_loop(0, td // DMA_UNROLL, drain, 0)

        @pl.when(i == pl.num_programs(0) - 1)
        def _():
            for e in range(n_exp):
                lo, hi = fill_ref[2 * e], fill_ref[2 * e + 1]

                def fill(r, carry):
                    copy(0, r).start()
                    return carry
                lax.fori_loop(lo, hi, fill, 0)

                def drain_one(r, carry):
                    copy(0, 0).wait()
                    return carry
                lax.fori_loop(lo, hi, drain_one, 0)

    @pl.when(i < na)
    def _():
        run(xa_ref)

    @pl.when(i >= na)
    def _():
        run(xb_ref)

    @pl.when(i == pl.num_programs(0) - 1)
    def _():
        zero_ref[...] = jnp.zeros_like(zero_ref)

        def zero_tile(t, carry):
            cp = pltpu.make_async_copy(zero_ref, o_ref.at[pl.ds(pl.multiple_of(t * tm, tm), tm)], zsem)
            cp.start()
            cp.wait()
            return carry
        lax.fori_loop(fill_ref[2 * n_exp], o_ref.shape[0] // tm, zero_tile, 0)


def _dispatch(xa, xb, pos, fill, rows_out, td, tm, n_exp):
    d = xa.shape[1]
    na, nb = xa.shape[0] // td, xb.shape[0] // td
    grid_spec = pltpu.PrefetchScalarGridSpec(
        num_scalar_prefetch=1,
        grid=(na + nb,),
        in_specs=[pl.BlockSpec((1, 1, 2 * td), lambda i, f: (i, 0, 0), memory_space=pltpu.SMEM),
                  pl.BlockSpec((td, d), lambda i, f: (jnp.minimum(i, na - 1), 0)),
                  pl.BlockSpec((td, d), lambda i, f: (jnp.maximum(i - na, 0), 0))],
        out_specs=pl.BlockSpec(memory_space=pl.ANY),
        scratch_shapes=[pltpu.VMEM((tm, d), F32), pltpu.SemaphoreType.DMA, pltpu.SemaphoreType.DMA],
    )
    return pl.pallas_call(
        functools.partial(_dispatch_kernel, na=na, td=td, n_exp=n_exp),
        grid_spec=grid_spec,
        out_shape=jax.ShapeDtypeStruct((rows_out, d), F32),
        compiler_params=_params(("arbitrary",)),
        name="moe_dispatch",
    )(fill, pos.reshape(na + nb, 1, 2 * td), xa, xb)


def _combine_kernel(pos_ref, posn_ref, xa_ref, xb_ref, gate_ref, g_ref, y_ref, oa_ref, ob_ref,
                    buf_ref, sem, *, na):
    i = pl.program_id(0)
    tc = xa_ref.shape[0]
    slot = lax.rem(i, 2)

    def copy(s, k, r, src):
        return pltpu.make_async_copy(y_ref.at[pl.ds(src, 1)], buf_ref.at[s, k, pl.ds(r, 1)],
                                     sem.at[s])

    def issue(p_ref, s):
        def body(g, carry):
            for u in range(DMA_UNROLL):
                r = g * DMA_UNROLL + u
                copy(s, 0, r, p_ref[0, 0, 2 * r]).start(priority=0)
                copy(s, 1, r, p_ref[0, 0, 2 * r + 1]).start(priority=1)
            return carry
        lax.fori_loop(0, tc // DMA_UNROLL, body, 0)

    @pl.when(i == 0)
    def _():
        issue(pos_ref, 0)

    @pl.when(i + 1 < pl.num_programs(0))
    def _():
        issue(posn_ref, 1 - slot)

    def drain(g, carry):
        for _ in range(2 * DMA_UNROLL):
            copy(slot, 0, 0, 0).wait()
        return carry
    lax.fori_loop(0, tc // DMA_UNROLL, drain, 0)

    x = _two_source(i, na, xa_ref, xb_ref)
    gate = gate_ref[...]
    out = _rms(x + (gate[:, 0:1] * buf_ref[slot, 0] + gate[:, 1:2] * buf_ref[slot, 1]), g_ref[...])

    @pl.when(i < na)
    def _():
        oa_ref[...] = out

    @pl.when(i >= na)
    def _():
        ob_ref[...] = out


def _combine(xa, xb, gates, pos, y_sorted, g_final, tc):
    d = xa.shape[1]
    na, nb = xa.shape[0] // tc, xb.shape[0] // tc
    n = na + nb
    amap = lambda i: (jnp.minimum(i, na - 1), 0)
    bmap = lambda i: (jnp.maximum(i - na, 0), 0)
    pos = pos.reshape(n, 1, 2 * tc)
    return pl.pallas_call(
        functools.partial(_combine_kernel, na=na),
        grid=(n,),
        in_specs=[pl.BlockSpec((1, 1, 2 * tc), lambda i: (i, 0, 0), memory_space=pltpu.SMEM),
                  pl.BlockSpec((1, 1, 2 * tc), lambda i: (jnp.minimum(i + 1, n - 1), 0, 0),
                               memory_space=pltpu.SMEM),
                  pl.BlockSpec((tc, d), amap), pl.BlockSpec((tc, d), bmap),
                  pl.BlockSpec((tc, 2), lambda i: (i, 0)),
                  pl.BlockSpec((1, d), lambda i: (0, 0)),
                  pl.BlockSpec(memory_space=pl.ANY)],
        out_specs=[pl.BlockSpec((tc, d), amap), pl.BlockSpec((tc, d), bmap)],
        out_shape=[jax.ShapeDtypeStruct(xa.shape, F32), jax.ShapeDtypeStruct(xb.shape, F32)],
        scratch_shapes=[pltpu.VMEM((2, 2, tc, d), F32), pltpu.SemaphoreType.DMA((2,))],
        compiler_params=_params(("arbitrary",)),
        name="moe_combine",
    )(pos, pos, xa, xb, gates, g_final.reshape(1, d), y_sorted)


def _moe(xa, xb, g_ffn, w_router, w_gu, w_down, g_final, tm, tf):
    n_exp = w_router.shape[1]
    rows = xa.shape[0] + xb.shape[0]
    eid, rank, gates, counts = _router(xa, xb, g_ffn, w_router, tr=512)
    counts = counts[0]
    padded = (counts + tm - 1) // tm * tm
    ends = jnp.cumsum(padded)
    offs = ends - padded
    pos = (offs[eid] + rank).reshape(-1)
    n_tiles = (2 * rows + n_exp * (tm - 1)) // tm
    n_used = (ends[-1] // tm).astype(I32)
    tile_start = jnp.arange(n_tiles, dtype=I32) * tm
    tile_expert = jnp.sum(tile_start[:, None] >= ends[None, :], axis=1).astype(I32)
    last = jnp.take(tile_expert, jnp.maximum(n_used - 1, 0))
    tile_expert = jnp.minimum(jnp.where(tile_start < ends[-1], tile_expert, last), n_exp - 1)
    fill = jnp.concatenate([jnp.stack([offs + counts, offs + padded], axis=1).reshape(-1),
                            n_used.reshape(1)]).astype(I32)
    x_sorted = _dispatch(xa, xb, pos, fill, n_tiles * tm, td=512, tm=tm, n_exp=n_exp)
    y_sorted = _ffn(x_sorted, g_ffn, w_gu, w_down, tile_expert, n_used.reshape(1), tm, tf,
                    residual=False)
    return _combine(xa, xb, gates, pos, y_sorted, g_final, tc=256)


def kernel(x_prompt, x_sample, mem_prompt, cache_swa_k, cache_swa_v, state_ssm_re, state_ssm_im, cache_mem_k, cache_mem_v, g_mix, g_ffn, g_mem, g_final, w_mem_kv, w_in_a, sinks, w_out_a, w_in_b, lam_re, lam_im, log_step, b_re, b_im, c_re, c_im, d_skip, w_glu, b_glu, w_out_b, w_ffn_gu, w_ffn_down, w_router, w_exp_gu, w_exp_down):
    n_batch, seq, d = x_prompt.shape
    n_dec, t_new, _ = x_sample.shape
    n_mem = mem_prompt.shape[1]
    depth = g_mix.shape[0]
    assert depth == 2 and w_in_a.shape[0] == 1 and w_in_b.shape[0] == 1
    n_kv = cache_swa_k.shape[3]
    n_q = sinks.shape[1]
    group = n_q // n_kv
    n_x = cache_mem_k.shape[3]
    x_q = n_x * HEAD_DIM
    kv_a = n_kv * HEAD_DIM
    q_a = n_q * HEAD_DIM
    s5_width = w_glu.shape[1]
    n_groups, n_state_g = lam_re.shape[1:]
    n_state = n_groups * n_state_g
    cache_w = cache_swa_k.shape[2]

    xp = x_prompt.reshape(n_batch * seq, d)
    xs = x_sample.reshape(n_dec * t_new, d)
    mem = mem_prompt.reshape(n_batch * n_mem, d)
    bt = 16
    chunk = 1024
    sub_len = chunk // SUBSEQ

    memkv = [_proj(mem, g_mem[i], w_mem_kv[i], tm=n_mem) for i in range(depth)]
    mem_k_prompt = jnp.stack([m[:, :x_q].reshape(n_batch, n_mem, n_x, HEAD_DIM) for m in memkv])
    mem_v_prompt = jnp.stack([m[:, x_q:].reshape(n_batch, n_mem, n_x, HEAD_DIM) for m in memkv])

    w_in = jnp.concatenate(
        [w_in_a[0][:, :q_a].reshape(d, n_kv, group, HEAD_DIM).transpose(0, 2, 1, 3).reshape(d, q_a),
         w_in_a[0][:, q_a:]], axis=1)
    w_out = jnp.concatenate(
        [w_out_a[0][:q_a].reshape(n_kv, group, HEAD_DIM, d).transpose(1, 0, 2, 3).reshape(q_a, d),
         w_out_a[0][q_a:]], axis=0)
    sink = sinks[0].reshape(n_kv, group).T.reshape(-1)
    pa_p = _proj(xp, g_mix[0], w_in, tm=512)
    pa_s = _proj(xs, g_mix[0], w_in, tm=512)
    x1p = _mixa_prompt(pa_p, xp, memkv[0], sink, w_out, n_batch, n_kv, group, n_x, n_mem)
    x1s, swa_ks, swa_vs = _mixa_sample(
        pa_s, xs, cache_swa_k[0].reshape(n_dec, cache_w, kv_a), cache_swa_v[0].reshape(n_dec, cache_w, kv_a),
        cache_mem_k[0].reshape(n_dec, n_mem, x_q), cache_mem_v[0].reshape(n_dec, n_mem, x_q),
        sink, w_out, n_kv, group, n_x, t_new, bt)
    w_keep = min(WINDOW, seq)
    kv_p = pa_p.reshape(n_batch, seq, -1)[:, seq - w_keep:, q_a:q_a + 2 * kv_a]
    swa_k_prompt = kv_p[..., :kv_a].reshape(1, n_batch, w_keep, n_kv, HEAD_DIM)
    swa_v_prompt = kv_p[..., kv_a:].reshape(1, n_batch, w_keep, n_kv, HEAD_DIM)
    swa_k_sample = swa_ks.reshape(1, n_dec, cache_w, n_kv, HEAD_DIM)
    swa_v_sample = swa_vs.reshape(1, n_dec, cache_w, n_kv, HEAD_DIM)

    x2p = _ffn(x1p, g_ffn[0], w_ffn_gu, w_ffn_down, jnp.zeros((x1p.shape[0] // 1024,), I32),
               jnp.full((1,), x1p.shape[0] // 1024, I32), 1024, 512, residual=True)
    x2s = _ffn(x1s, g_ffn[0], w_ffn_gu, w_ffn_down, jnp.zeros((x1s.shape[0] // 1024,), I32),
               jnp.full((1,), x1s.shape[0] // 1024, I32), 1024, 512, residual=True)

    pb_p = _proj(x2p, g_mix[1], w_in_b[0], tm=512)
    pb_s = _proj(x2s, g_mix[1], w_in_b[0], tm=512)
    sw = _s5_weights(lam_re[0], lam_im[0], log_step[0], b_re[0], b_im[0], c_re[0], c_im[0], sub_len)
    nchunk = seq // chunk
    u_perm = (pb_p[:, :s5_width].astype(BF16)
              .reshape(n_batch, nchunk, SUBSEQ, sub_len, s5_width)
              .transpose(0, 1, 3, 2, 4).reshape(n_batch * seq, s5_width))
    so_perm, h_fin = _s5_prompt(u_perm, sw, d_skip[0], w_glu[0], b_glu[0], n_batch, chunk)
    so_p = (so_perm.reshape(n_batch, nchunk, sub_len, SUBSEQ, s5_width)
            .transpose(0, 1, 3, 2, 4).reshape(n_batch * seq, s5_width))
    ssm_re_prompt = h_fin[:, 0, :n_state].reshape(1, n_batch, n_groups, n_state_g)
    ssm_im_prompt = h_fin[:, 0, n_state:].reshape(1, n_batch, n_groups, n_state_g)

    u_tm = (pb_s[:, :s5_width].astype(BF16).reshape(n_dec, t_new, s5_width)
            .transpose(1, 0, 2).reshape(n_dec * t_new, s5_width))
    so_tm, hr_s, hi_s = _s5_sample(u_tm, state_ssm_re[0].reshape(n_dec, n_state),
                                   state_ssm_im[0].reshape(n_dec, n_state), sw,
                                   d_skip[0], w_glu[0], b_glu[0], t_new)
    so_s = so_tm.reshape(t_new, n_dec, s5_width).transpose(1, 0, 2).reshape(n_dec * t_new, s5_width)
    ssm_re_sample = hr_s.reshape(1, n_dec, n_groups, n_state_g)
    ssm_im_sample = hi_s.reshape(1, n_dec, n_groups, n_state_g)

    x3p = _tailb_prompt(so_p, pb_p, x2p, memkv[1], w_out_b[0], n_batch, n_x, n_mem, tm=256)
    x3s = _tailb_sample(so_s, pb_s, x2s, cache_mem_k[1].reshape(n_dec, n_mem, x_q),
                        cache_mem_v[1].reshape(n_dec, n_mem, x_q), w_out_b[0], n_x, t_new, bt)

    yp, ys = _moe(x3p, x3s, g_ffn[1], w_router[0], w_exp_gu[0], w_exp_down[0], g_final, 1024, 512)
    return (yp.reshape(n_batch, seq, d), ys.reshape(n_dec, t_new, d),
            swa_k_prompt, swa_v_prompt, ssm_re_prompt, ssm_im_prompt,
            mem_k_prompt, mem_v_prompt, swa_k_sample, swa_v_sample,
            ssm_re_sample, ssm_im_sample)
```

```python
import functools

import jax
import jax.numpy as jnp
from jax import lax
from jax.experimental import pallas as pl
from jax.experimental.pallas import tpu as pltpu

F32 = jnp.float32
BF16 = jnp.bfloat16
I32 = jnp.int32

HEAD_DIM = 64
WINDOW = 128
BLOCK = 128
EPS = 1e-5
SCALE = HEAD_DIM ** -0.5
S5_GROUP = 16
SUBSEQ = 8
SUBLANES = 8
LANE_WIN = 512
VMEM_LIMIT = 56 * 1024 * 1024

_NT = (((1,), (1,)), ((), ()))


def _params(sem, vmem=VMEM_LIMIT):
    return pltpu.CompilerParams(dimension_semantics=sem, vmem_limit_bytes=vmem)


def _rms(x, g):
    return x * lax.rsqrt(jnp.mean(x * x, axis=-1, keepdims=True) + EPS) * g


def _softmax(s):
    m = jnp.max(s, axis=-1, keepdims=True)
    p = jnp.exp(s - m)
    return p / jnp.sum(p, axis=-1, keepdims=True)


def _proj_kernel(x_ref, g_ref, w_ref, o_ref):
    h = _rms(x_ref[...], g_ref[...]).astype(BF16)
    o_ref[...] = jnp.dot(h, w_ref[...], preferred_element_type=F32)


def _proj(x, g, w, tm):
    rows, d = x.shape
    n = w.shape[1]
    return pl.pallas_call(
        _proj_kernel,
        grid=(rows // tm,),
        in_specs=[pl.BlockSpec((tm, d), lambda i: (i, 0)),
                  pl.BlockSpec((1, d), lambda i: (0, 0)),
                  pl.BlockSpec((d, n), lambda i: (0, 0))],
        out_specs=pl.BlockSpec((tm, n), lambda i: (i, 0)),
        out_shape=jax.ShapeDtypeStruct((rows, n), F32),
        compiler_params=_params(("parallel",)),
        name="rms_proj",
    )(x, g.reshape(1, d), w.astype(BF16))


def _sink_attention_group(q_rows, k_h, v_h, mask, sink_col):
    s = lax.dot_general(q_rows, k_h, _NT, preferred_element_type=F32)
    s = jnp.where(mask, s, -jnp.inf)
    m = jnp.maximum(jnp.max(s, axis=-1, keepdims=True), sink_col)
    p = jnp.exp(s - m)
    den = jnp.sum(p, axis=-1, keepdims=True) + jnp.exp(sink_col - m)
    return jnp.dot((p / den).astype(BF16), v_h, preferred_element_type=F32)


def _cross_attention(xq, mk, mv, n_heads, mix_ref, rows, col0):
    xq = (xq * SCALE).astype(BF16)
    for h in range(n_heads):
        sl = slice(h * HEAD_DIM, (h + 1) * HEAD_DIM)
        s = lax.dot_general(xq[:, sl], mk[:, sl], _NT, preferred_element_type=F32)
        o = jnp.dot(_softmax(s).astype(BF16), mv[:, sl], preferred_element_type=F32)
        mix_ref[rows, col0 + h * HEAD_DIM:col0 + (h + 1) * HEAD_DIM] = o


def _sink_column(sink_ref, heads, rows_per_head):
    n = len(heads)
    row = lax.broadcasted_iota(I32, (n * rows_per_head, 1), 0)
    col = jnp.full((n * rows_per_head, 1), sink_ref[0, heads[n - 1]], F32)
    for g in range(n - 2, -1, -1):
        col = jnp.where(row < (g + 1) * rows_per_head, sink_ref[0, heads[g]], col)
    return col


def _head_mask(n_heads, t):
    shape = (n_heads * t, n_heads * HEAD_DIM)
    row = lax.broadcasted_iota(I32, shape, 0)
    col = lax.broadcasted_iota(I32, shape, 1)
    lo = (row // t) * HEAD_DIM
    return (col >= lo) & (col < lo + HEAD_DIM)


def _spread_heads(x, hmask, n_heads):
    return jnp.where(hmask, jnp.concatenate([x] * n_heads, axis=0), 0.0).astype(BF16)


def _gather_heads(o, hmask, n_heads, t):
    o = jnp.where(hmask, o, 0.0)
    acc = o[0:t]
    for h in range(1, n_heads):
        acc = acc + o[h * t:(h + 1) * t]
    return acc


def _cross_attention_small(xq, mk, mv, hmask, n_heads):
    t = xq.shape[0]
    s = lax.dot_general(_spread_heads(xq * SCALE, hmask, n_heads), mk, _NT,
                        preferred_element_type=F32)
    o = jnp.dot(_softmax(s).astype(BF16), mv, preferred_element_type=F32)
    return _gather_heads(o, hmask, n_heads, t)


def _out_proj(x_ref, mix_ref, wo_ref, o_ref):
    o_ref[...] = x_ref[...] + jnp.dot(mix_ref[...].astype(BF16), wo_ref[...],
                                      preferred_element_type=F32)


def _mixa_prompt_kernel(sink_ref, q_ref, kp_ref, kc_ref, vp_ref, vc_ref, xq_ref, x_ref,
                        mk_ref, mv_ref, wo_ref, o_ref, mix_ref, *, n_kv, group, n_x):
    i = pl.program_id(1)
    q = q_ref[...] * SCALE
    m_rows = group * BLOCK
    qi = lax.rem(lax.broadcasted_iota(I32, (m_rows, 2 * BLOCK), 0), BLOCK)
    kj = lax.broadcasted_iota(I32, (m_rows, 2 * BLOCK), 1)
    dist = qi + BLOCK - kj
    band = (dist >= 0) & (dist <= WINDOW)
    first = jnp.where(i > 0, 0, BLOCK)
    for sb in range(q_ref.shape[0] // BLOCK):
        rows = slice(sb * BLOCK, (sb + 1) * BLOCK)
        before = slice((sb - 1) * BLOCK, sb * BLOCK)
        k_prev = kp_ref[...] if sb == 0 else kc_ref[before, :]
        v_prev = vp_ref[...] if sb == 0 else vc_ref[before, :]
        kk = jnp.concatenate([k_prev, kc_ref[rows, :]], axis=0).astype(BF16)
        vv = jnp.concatenate([v_prev, vc_ref[rows, :]], axis=0).astype(BF16)
        mask = (band & (kj >= first)) if sb == 0 else band
        for h in range(n_kv):
            sl = slice(h * HEAD_DIM, (h + 1) * HEAD_DIM)
            cols = [(g * n_kv + h) * HEAD_DIM for g in range(group)]
            qh = jnp.concatenate([q[rows, c0:c0 + HEAD_DIM] for c0 in cols], axis=0).astype(BF16)
            o = _sink_attention_group(
                qh, kk[:, sl], vv[:, sl], mask,
                _sink_column(sink_ref, [g * n_kv + h for g in range(group)], BLOCK))
            for g, c0 in enumerate(cols):
                mix_ref[rows, c0:c0 + HEAD_DIM] = o[g * BLOCK:(g + 1) * BLOCK]
    q_a = n_kv * group * HEAD_DIM
    _cross_attention(xq_ref[...], mk_ref[...].astype(BF16), mv_ref[...].astype(BF16), n_x,
                     mix_ref, slice(None), q_a)
    _out_proj(x_ref, mix_ref, wo_ref, o_ref)


def _mixa_prompt(p, x, memkv, sinks, w_out, n_batch, n_kv, group, n_x, n_mem):
    rows, d = x.shape
    nb = rows // n_batch // BLOCK
    q_a = n_kv * group * HEAD_DIM
    kv_a = n_kv * HEAD_DIM
    x_q = n_x * HEAD_DIM
    kcol = q_a // kv_a
    vcol = (q_a + kv_a) // kv_a
    xcol = (q_a + 2 * kv_a) // x_q
    qt = 2
    tm = qt * BLOCK
    nt = nb // qt
    cur = lambda b, i: b * nt + i
    prev = lambda b, i: b * nb + jnp.maximum(i * qt - 1, 0)
    kern = functools.partial(_mixa_prompt_kernel, n_kv=n_kv, group=group, n_x=n_x)
    return pl.pallas_call(
        kern,
        grid=(n_batch, nt),
        in_specs=[
            pl.BlockSpec(memory_space=pltpu.SMEM),
            pl.BlockSpec((tm, q_a), lambda b, i: (cur(b, i), 0)),
            pl.BlockSpec((BLOCK, kv_a), lambda b, i: (prev(b, i), kcol)),
            pl.BlockSpec((tm, kv_a), lambda b, i: (cur(b, i), kcol)),
            pl.BlockSpec((BLOCK, kv_a), lambda b, i: (prev(b, i), vcol)),
            pl.BlockSpec((tm, kv_a), lambda b, i: (cur(b, i), vcol)),
            pl.BlockSpec((tm, x_q), lambda b, i: (cur(b, i), xcol)),
            pl.BlockSpec((tm, d), lambda b, i: (cur(b, i), 0)),
            pl.BlockSpec((n_mem, x_q), lambda b, i: (b, 0)),
            pl.BlockSpec((n_mem, x_q), lambda b, i: (b, 1)),
            pl.BlockSpec((q_a + x_q, d), lambda b, i: (0, 0)),
        ],
        out_specs=pl.BlockSpec((tm, d), lambda b, i: (cur(b, i), 0)),
        out_shape=jax.ShapeDtypeStruct((rows, d), F32),
        scratch_shapes=[pltpu.VMEM((tm, q_a + x_q), F32)],
        compiler_params=_params(("parallel", "parallel")),
        name="mix_a_prompt",
    )(sinks.reshape(1, -1), p, p, p, p, p, p, x, memkv, memkv, w_out.astype(BF16))


def _mixa_sample_kernel(sink_ref, p_ref, x_ref, ck_ref, cv_ref, mk_ref, mv_ref, wo_ref,
                        o_ref, ok_ref, ov_ref, mix_ref, *, n_kv, group, n_x, t_new, bt):
    w = ck_ref.shape[1]
    q_a = n_kv * group * HEAD_DIM
    kv_a = n_kv * HEAD_DIM
    m_rows = group * n_kv * t_new
    ti = lax.rem(lax.broadcasted_iota(I32, (m_rows, w + t_new), 0), t_new)
    kj = lax.broadcasted_iota(I32, (m_rows, w + t_new), 1)
    dist = ti + w - kj
    mask = (dist >= 0) & (dist <= WINDOW)
    kv_mask = _head_mask(n_kv, t_new)
    x_mask = _head_mask(n_x, t_new)
    sink_col = sink_ref[...]

    def body(bb, carry):
        rows = pl.ds(pl.multiple_of(bb * t_new, t_new), t_new)
        q = p_ref[rows, 0:q_a] * SCALE
        k_new = p_ref[rows, q_a:q_a + kv_a]
        v_new = p_ref[rows, q_a + kv_a:q_a + 2 * kv_a]
        xq = p_ref[rows, q_a + 2 * kv_a:]
        k_old = ck_ref[bb]
        v_old = cv_ref[bb]
        ok_ref[bb, 0:w - t_new, :] = k_old[t_new:, :]
        ok_ref[bb, w - t_new:w, :] = k_new
        ov_ref[bb, 0:w - t_new, :] = v_old[t_new:, :]
        ov_ref[bb, w - t_new:w, :] = v_new
        kk = jnp.concatenate([k_old, k_new], axis=0).astype(BF16)
        vv = jnp.concatenate([v_old, v_new], axis=0).astype(BF16)
        qs = jnp.concatenate([_spread_heads(q[:, g * kv_a:(g + 1) * kv_a], kv_mask, n_kv)
                              for g in range(group)], axis=0)
        o = _sink_attention_group(qs, kk, vv, mask, sink_col)
        blk = n_kv * t_new
        for g in range(group):
            mix_ref[rows, g * kv_a:(g + 1) * kv_a] = _gather_heads(
                o[g * blk:(g + 1) * blk], kv_mask, n_kv, t_new)
        mix_ref[rows, q_a:] = _cross_attention_small(
            xq, mk_ref[bb].astype(BF16), mv_ref[bb].astype(BF16), x_mask, n_x)
        return carry

    lax.fori_loop(0, bt, body, 0, unroll=2)
    _out_proj(x_ref, mix_ref, wo_ref, o_ref)


def _mixa_sample(p, x, cache_k, cache_v, mem_k, mem_v, sinks, w_out, n_kv, group, n_x, t_new, bt):
    rows, d = x.shape
    n_dec, w, kv_a = cache_k.shape
    n_mem, x_q = mem_k.shape[1:]
    q_a = n_kv * group * HEAD_DIM
    tm = bt * t_new
    kern = functools.partial(_mixa_sample_kernel, n_kv=n_kv, group=group, n_x=n_x,
                             t_new=t_new, bt=bt)
    return pl.pallas_call(
        kern,
        grid=(n_dec // bt,),
        in_specs=[
            pl.BlockSpec((n_kv * group * t_new, 1), lambda i: (0, 0)),
            pl.BlockSpec((tm, p.shape[1]), lambda i: (i, 0)),
            pl.BlockSpec((tm, d), lambda i: (i, 0)),
            pl.BlockSpec((bt, w, kv_a), lambda i: (i, 0, 0)),
            pl.BlockSpec((bt, w, kv_a), lambda i: (i, 0, 0)),
            pl.BlockSpec((bt, n_mem, x_q), lambda i: (i, 0, 0)),
            pl.BlockSpec((bt, n_mem, x_q), lambda i: (i, 0, 0)),
            pl.BlockSpec((q_a + x_q, d), lambda i: (0, 0)),
        ],
        out_specs=[pl.BlockSpec((tm, d), lambda i: (i, 0)),
                   pl.BlockSpec((bt, w, kv_a), lambda i: (i, 0, 0)),
                   pl.BlockSpec((bt, w, kv_a), lambda i: (i, 0, 0))],
        out_shape=[jax.ShapeDtypeStruct((rows, d), F32),
                   jax.ShapeDtypeStruct((n_dec, w, kv_a), F32),
                   jax.ShapeDtypeStruct((n_dec, w, kv_a), F32)],
        scratch_shapes=[pltpu.VMEM((tm, q_a + x_q), F32)],
        compiler_params=_params(("parallel",)),
        name="mix_a_sample",
    )(jnp.repeat(sinks, t_new).reshape(-1, 1), p, x, cache_k, cache_v, mem_k, mem_v,
      w_out.astype(BF16))


def _tailb_prompt_kernel(s_ref, xq_ref, x_ref, mk_ref, mv_ref, wo_ref, o_ref, mix_ref, *, n_x):
    width = s_ref.shape[1]
    mix_ref[:, 0:width] = s_ref[...].astype(F32)
    _cross_attention(xq_ref[...], mk_ref[...].astype(BF16), mv_ref[...].astype(BF16), n_x,
                     mix_ref, slice(None), width)
    _out_proj(x_ref, mix_ref, wo_ref, o_ref)


def _tailb_prompt(s_out, p, x, memkv, w_out, n_batch, n_x, n_mem, tm):
    rows, d = x.shape
    width = s_out.shape[1]
    x_q = n_x * HEAD_DIM
    per_b = rows // n_batch // tm
    kern = functools.partial(_tailb_prompt_kernel, n_x=n_x)
    return pl.pallas_call(
        kern,
        grid=(rows // tm,),
        in_specs=[
            pl.BlockSpec((tm, width), lambda i: (i, 0)),
            pl.BlockSpec((tm, x_q), lambda i: (i, width // x_q)),
            pl.BlockSpec((tm, d), lambda i: (i, 0)),
            pl.BlockSpec((n_mem, x_q), lambda i: (i // per_b, 0)),
            pl.BlockSpec((n_mem, x_q), lambda i: (i // per_b, 1)),
            pl.BlockSpec((width + x_q, d), lambda i: (0, 0)),
        ],
        out_specs=pl.BlockSpec((tm, d), lambda i: (i, 0)),
        out_shape=jax.ShapeDtypeStruct((rows, d), F32),
        scratch_shapes=[pltpu.VMEM((tm, width + x_q), F32)],
        compiler_params=_params(("parallel",)),
        name="tail_b_prompt",
    )(s_out, p, x, memkv, memkv, w_out.astype(BF16))


def _tailb_sample_kernel(s_ref, xq_ref, x_ref, mk_ref, mv_ref, wo_ref, o_ref, mix_ref,
                         *, n_x, t_new, bt):
    width = s_ref.shape[1]
    mix_ref[:, 0:width] = s_ref[...].astype(F32)

    x_mask = _head_mask(n_x, t_new)

    def body(bb, carry):
        rows = pl.ds(pl.multiple_of(bb * t_new, t_new), t_new)
        mix_ref[rows, width:] = _cross_attention_small(
            xq_ref[rows, :], mk_ref[bb].astype(BF16), mv_ref[bb].astype(BF16), x_mask, n_x)
        return carry

    lax.fori_loop(0, bt, body, 0, unroll=2)
    _out_proj(x_ref, mix_ref, wo_ref, o_ref)


def _tailb_sample(s_out, p, x, mem_k, mem_v, w_out, n_x, t_new, bt):
    rows, d = x.shape
    width = s_out.shape[1]
    n_dec, n_mem, x_q = mem_k.shape
    tm = bt * t_new
    kern = functools.partial(_tailb_sample_kernel, n_x=n_x, t_new=t_new, bt=bt)
    return pl.pallas_call(
        kern,
        grid=(n_dec // bt,),
        in_specs=[
            pl.BlockSpec((tm, width), lambda i: (i, 0)),
            pl.BlockSpec((tm, x_q), lambda i: (i, width // x_q)),
            pl.BlockSpec((tm, d), lambda i: (i, 0)),
            pl.BlockSpec((bt, n_mem, x_q), lambda i: (i, 0, 0)),
            pl.BlockSpec((bt, n_mem, x_q), lambda i: (i, 0, 0)),
            pl.BlockSpec((width + x_q, d), lambda i: (0, 0)),
        ],
        out_specs=pl.BlockSpec((tm, d), lambda i: (i, 0)),
        out_shape=jax.ShapeDtypeStruct((rows, d), F32),
        scratch_shapes=[pltpu.VMEM((tm, width + x_q), F32)],
        compiler_params=_params(("parallel",)),
        name="tail_b_sample",
    )(s_out, p, x, mem_k, mem_v, w_out.astype(BF16))


def _s5_disc_kernel(lr_ref, li_ref, ls_ref, br_ref, bi_ref,
                    ar_ref, ai_ref, apr_ref, api_ref, bbr_ref, bbi_ref, *, n_square):
    lr, li = lr_ref[...], li_ref[...]
    step = jnp.exp(ls_ref[...])
    mag = jnp.exp(lr * step)
    ar, ai = mag * jnp.cos(li * step), mag * jnp.sin(li * step)
    den = lr * lr + li * li
    cr = ((ar - 1.0) * lr + ai * li) / den
    ci = (ai * lr - (ar - 1.0) * li) / den
    br, bi = br_ref[...], bi_ref[...]
    bbr_ref[...] = cr * br - ci * bi
    bbi_ref[...] = cr * bi + ci * br
    ar_ref[...] = ar
    ai_ref[...] = ai
    pr, pi = ar, ai
    for _ in range(n_square):
        pr, pi = pr * pr - pi * pi, 2.0 * pr * pi
    apr_ref[...] = pr
    api_ref[...] = pi


def _s5_weights(lam_re, lam_im, log_step, b_re, b_im, c_re, c_im, sub_len):
    g, p = lam_re.shape
    hh = b_re.shape[2]
    rep = lambda a: jnp.repeat(a, hh, axis=0)
    flat = lambda a: a.transpose(0, 2, 1).reshape(g * hh, p)
    n_square = sub_len.bit_length() - 1
    assert 1 << n_square == sub_len
    shp = jax.ShapeDtypeStruct((g * hh, p), F32)
    ar, ai, apr, api, bbr, bbi = pl.pallas_call(
        functools.partial(_s5_disc_kernel, n_square=n_square),
        out_shape=[shp] * 6,
        name="s5_discretise",
    )(rep(lam_re), rep(lam_im), rep(jnp.broadcast_to(log_step[:, None], (g, p))),
      flat(b_re), flat(b_im))
    row = lambda a: a[::hh].reshape(1, g * p)
    gb = 256 // hh
    nblk = g // gb
    eye = jnp.eye(gb, dtype=F32)
    bd_in = lambda a: (a.reshape(nblk, gb, hh, 1, p) * eye[None, :, None, :, None]
                       ).reshape(nblk, gb * hh, gb * p)
    wb = jnp.concatenate([bd_in(bbr), bd_in(bbi)], axis=-1).astype(BF16)
    bd_out = lambda c: (c.reshape(nblk, gb, hh, p).transpose(0, 1, 3, 2)[:, :, :, None, :]
                        * eye[None, :, None, :, None]).reshape(nblk, gb * p, gb * hh)
    wc = jnp.concatenate([bd_out(c_re), -bd_out(c_im)], axis=1).astype(BF16)
    return row(ar), row(ai), row(apr), row(api), wb, wc


def _cstep(ar, ai, hr, hi, xr, xi):
    return ar * hr - ai * hi + xr, ar * hi + ai * hr + xi


def _glu_out(y, u, d_ref, wg_ref, bg_ref):
    z = jax.nn.gelu(y + d_ref[...] * u)
    gate = jnp.dot(z.astype(BF16), wg_ref[...], preferred_element_type=F32) + bg_ref[...]
    return z * jax.nn.sigmoid(gate)


def _s5_prompt_kernel(u_ref, ar_ref, ai_ref, apr_ref, api_ref, wb_ref, wc_ref, d_ref, wg_ref,
                      bg_ref, o_ref, hf_ref, xs_ref, y_ref, carry_ref, *, n_steps):
    c = pl.program_id(1)
    nblk, ublk, two_cw = wb_ref.shape
    cw = two_cw // 2
    n_state = nblk * cw

    @pl.when(c == 0)
    def _():
        carry_ref[...] = jnp.zeros_like(carry_ref)

    row8 = lax.broadcasted_iota(I32, (SUBLANES, LANE_WIN), 0)
    for j in range(nblk):
        xs_ref[...] = jnp.dot(u_ref[:, j * ublk:(j + 1) * ublk], wb_ref[j],
                              preferred_element_type=F32)
        for lb in range(cw // LANE_WIN):
            re = slice(lb * LANE_WIN, (lb + 1) * LANE_WIN)
            im = slice(cw + lb * LANE_WIN, cw + (lb + 1) * LANE_WIN)
            gre = slice(j * cw + lb * LANE_WIN, j * cw + (lb + 1) * LANE_WIN)
            gim = slice(n_state + j * cw + lb * LANE_WIN, n_state + j * cw + (lb + 1) * LANE_WIN)
            bc = lambda r: jnp.broadcast_to(r[:, gre], (SUBLANES, LANE_WIN))
            ar, ai, apr, api = bc(ar_ref), bc(ai_ref), bc(apr_ref), bc(api_ref)
            c_r = jnp.broadcast_to(carry_ref[:, gre], (SUBLANES, LANE_WIN))
            c_i = jnp.broadcast_to(carry_ref[:, gim], (SUBLANES, LANE_WIN))

            def rows(i):
                return pl.ds(pl.multiple_of(i * SUBLANES, SUBLANES), SUBLANES)

            def pass1(i, h):
                return _cstep(ar, ai, h[0], h[1], xs_ref[rows(i), re], xs_ref[rows(i), im])

            zero = jnp.zeros((SUBLANES, LANE_WIN), F32)
            e_r, e_i = lax.fori_loop(0, n_steps, pass1, (zero, zero), unroll=8)
            s_r = jnp.where(row8 == 0, c_r, 0.0)
            s_i = jnp.where(row8 == 0, c_i, 0.0)
            for _ in range(SUBSEQ - 1):
                n_r, n_i = _cstep(apr, api, s_r, s_i, e_r, e_i)
                s_r = jnp.where(row8 == 0, c_r, pltpu.roll(n_r, 1, 0))
                s_i = jnp.where(row8 == 0, c_i, pltpu.roll(n_i, 1, 0))
            n_r, n_i = _cstep(apr, api, s_r, s_i, e_r, e_i)
            carry_ref[:, gre] = n_r[SUBSEQ - 1:SUBSEQ, :]
            carry_ref[:, gim] = n_i[SUBSEQ - 1:SUBSEQ, :]

            def pass2(i, h):
                h = _cstep(ar, ai, h[0], h[1], xs_ref[rows(i), re], xs_ref[rows(i), im])
                xs_ref[rows(i), re] = h[0]
                xs_ref[rows(i), im] = h[1]
                return h

            lax.fori_loop(0, n_steps, pass2, (s_r, s_i), unroll=4)
        y_ref[:, j * ublk:(j + 1) * ublk] = jnp.dot(xs_ref[...].astype(BF16), wc_ref[j],
                                                    preferred_element_type=F32)
    o_ref[...] = _glu_out(y_ref[...], u_ref[...].astype(F32), d_ref, wg_ref, bg_ref).astype(o_ref.dtype)
    hf_ref[0] = carry_ref[...]


def _s5_prompt(u_perm, sw, d_skip, w_glu, b_glu, n_batch, chunk):
    rows, width = u_perm.shape
    ar, ai, apr, api, wb, wc = sw
    nblk, ublk, two_cw = wb.shape
    n_state = ar.shape[1]
    nchunk = rows // n_batch // chunk
    const2 = lambda b, c: (0, 0)
    const3 = lambda b, c: (0, 0, 0)
    kern = functools.partial(_s5_prompt_kernel, n_steps=chunk // SUBSEQ)
    return pl.pallas_call(
        kern,
        grid=(n_batch, nchunk),
        in_specs=[
            pl.BlockSpec((chunk, width), lambda b, c: (b * nchunk + c, 0)),
            pl.BlockSpec((1, n_state), const2), pl.BlockSpec((1, n_state), const2),
            pl.BlockSpec((1, n_state), const2), pl.BlockSpec((1, n_state), const2),
            pl.BlockSpec(wb.shape, const3), pl.BlockSpec(wc.shape, const3),
            pl.BlockSpec((1, width), const2), pl.BlockSpec((width, width), const2),
            pl.BlockSpec((1, width), const2),
        ],
        out_specs=[pl.BlockSpec((chunk, width), lambda b, c: (b * nchunk + c, 0)),
                   pl.BlockSpec((1, 1, 2 * n_state), lambda b, c: (b, 0, 0))],
        out_shape=[jax.ShapeDtypeStruct((rows, width), BF16),
                   jax.ShapeDtypeStruct((n_batch, 1, 2 * n_state), F32)],
        scratch_shapes=[pltpu.VMEM((chunk, two_cw), F32), pltpu.VMEM((chunk, width), F32),
                        pltpu.VMEM((1, 2 * n_state), F32)],
        compiler_params=_params(("arbitrary", "arbitrary")),
        name="s5_prompt",
    )(u_perm, ar, ai, apr, api, wb, wc, d_skip.reshape(1, -1), w_glu.astype(BF16),
      b_glu.reshape(1, -1))


def _s5_sample_kernel(u_ref, h0r_ref, h0i_ref, ar_ref, ai_ref, wb_ref, wc_ref, d_ref, wg_ref,
                      bg_ref, o_ref, hr_ref, hi_ref, xs_ref, y_ref, *, t_new):
    nblk, ublk, two_cw = wb_ref.shape
    cw = two_cw // 2
    n_seq = h0r_ref.shape[0]
    for j in range(nblk):
        xs_ref[...] = jnp.dot(u_ref[:, j * ublk:(j + 1) * ublk], wb_ref[j],
                              preferred_element_type=F32)
        for lb in range(cw // LANE_WIN):
            re = slice(lb * LANE_WIN, (lb + 1) * LANE_WIN)
            im = slice(cw + lb * LANE_WIN, cw + (lb + 1) * LANE_WIN)
            gre = slice(j * cw + lb * LANE_WIN, j * cw + (lb + 1) * LANE_WIN)
            ar = jnp.broadcast_to(ar_ref[:, gre], (SUBLANES, LANE_WIN))
            ai = jnp.broadcast_to(ai_ref[:, gre], (SUBLANES, LANE_WIN))

            def group_body(rg, carry):
                base = pl.multiple_of(rg * SUBLANES, SUBLANES)
                h = (h0r_ref[pl.ds(base, SUBLANES), gre], h0i_ref[pl.ds(base, SUBLANES), gre])
                for t in range(t_new):
                    r = pl.ds(t * n_seq + base, SUBLANES)
                    h = _cstep(ar, ai, h[0], h[1], xs_ref[r, re], xs_ref[r, im])
                    xs_ref[r, re] = h[0]
                    xs_ref[r, im] = h[1]
                hr_ref[pl.ds(base, SUBLANES), gre] = h[0]
                hi_ref[pl.ds(base, SUBLANES), gre] = h[1]
                return carry

            lax.fori_loop(0, n_seq // SUBLANES, group_body, 0)
        y_ref[:, j * ublk:(j + 1) * ublk] = jnp.dot(xs_ref[...].astype(BF16), wc_ref[j],
                                                    preferred_element_type=F32)
    o_ref[...] = _glu_out(y_ref[...], u_ref[...].astype(F32), d_ref, wg_ref, bg_ref).astype(o_ref.dtype)


def _s5_sample(u_tm, h0r, h0i, sw, d_skip, w_glu, b_glu, t_new):
    rows, width = u_tm.shape
    ar, ai, _, _, wb, wc = sw
    two_cw = wb.shape[2]
    n_seq, n_state = h0r.shape
    kern = functools.partial(_s5_sample_kernel, t_new=t_new)
    return pl.pallas_call(
        kern,
        out_shape=[jax.ShapeDtypeStruct((rows, width), BF16),
                   jax.ShapeDtypeStruct((n_seq, n_state), F32),
                   jax.ShapeDtypeStruct((n_seq, n_state), F32)],
        scratch_shapes=[pltpu.VMEM((rows, two_cw), F32), pltpu.VMEM((rows, width), F32)],
        compiler_params=pltpu.CompilerParams(vmem_limit_bytes=VMEM_LIMIT),
        name="s5_sample",
    )(u_tm, h0r, h0i, ar, ai, wb, wc, d_skip.reshape(1, -1), w_glu.astype(BF16),
      b_glu.reshape(1, -1))


def _ffn_kernel(te_ref, nu_ref, x_ref, g_ref, wg_ref, wu_ref, wd_ref, o_ref, hn_ref, acc_ref,
                *, residual):
    i, j = pl.program_id(0), pl.program_id(1)

    @pl.when(i < nu_ref[0])
    def _():
        @pl.when(j == 0)
        def _():
            hn_ref[...] = _rms(x_ref[...], g_ref[...]).astype(BF16)
            acc_ref[...] = jnp.zeros_like(acc_ref)

        h = hn_ref[...]
        gate = jnp.dot(h, wg_ref[0].astype(BF16), preferred_element_type=F32)
        up = jnp.dot(h, wu_ref[0].astype(BF16), preferred_element_type=F32)
        act = (gate * jax.nn.sigmoid(gate) * up).astype(BF16)
        acc_ref[...] += jnp.dot(act, wd_ref[0].astype(BF16), preferred_element_type=F32)

        @pl.when(j == pl.num_programs(1) - 1)
        def _():
            o_ref[...] = (x_ref[...] + acc_ref[...]) if residual else acc_ref[...]

    @pl.when((i >= nu_ref[0]) & (j == 0))
    def _():
        o_ref[...] = jnp.zeros_like(o_ref)


def _ffn(x, g, w_gu, w_down, tile_expert, n_used, tm, tf, residual):
    rows, d = x.shape
    d_ff = w_down.shape[1]
    nf = d_ff // tf
    row_map = lambda i, j, te, nu: (jnp.maximum(jnp.minimum(i, nu[0] - 1), 0), 0)
    jj = lambda i, j, nu: jnp.where(i < nu[0], j, nf - 1)
    kern = functools.partial(_ffn_kernel, residual=residual)
    grid_spec = pltpu.PrefetchScalarGridSpec(
        num_scalar_prefetch=2,
        grid=(rows // tm, nf),
        in_specs=[
            pl.BlockSpec((tm, d), row_map),
            pl.BlockSpec((1, d), lambda i, j, te, nu: (0, 0)),
            pl.BlockSpec((1, d, tf), lambda i, j, te, nu: (te[i], 0, jj(i, j, nu))),
            pl.BlockSpec((1, d, tf), lambda i, j, te, nu: (te[i], 0, nf + jj(i, j, nu))),
            pl.BlockSpec((1, tf, d), lambda i, j, te, nu: (te[i], jj(i, j, nu), 0)),
        ],
        out_specs=pl.BlockSpec((tm, d), lambda i, j, te, nu: (i, 0)),
        scratch_shapes=[pltpu.VMEM((tm, d), BF16), pltpu.VMEM((tm, d), F32)],
    )
    return pl.pallas_call(
        kern,
        grid_spec=grid_spec,
        out_shape=jax.ShapeDtypeStruct((rows, d), F32),
        compiler_params=_params(("arbitrary", "arbitrary")),
        name="swiglu_ffn",
    )(tile_expert, n_used, x, g.reshape(1, d), w_gu, w_gu, w_down)


def _two_source(i, na, a_ref, b_ref):
    return jnp.where(i < na, a_ref[...], b_ref[...])


def _router_kernel(xa_ref, xb_ref, g_ref, wr_ref, eid_ref, rank_ref, gate_ref, cnt_ref,
                   carry_ref, *, na):
    i = pl.program_id(0)
    tr = xa_ref.shape[0]
    n_exp = wr_ref.shape[1]

    @pl.when(i == 0)
    def _():
        carry_ref[...] = jnp.zeros_like(carry_ref)

    h = _rms(_two_source(i, na, xa_ref, xb_ref), g_ref[...])
    w = wr_ref[...]
    h_hi, w_hi = h.astype(BF16), w.astype(BF16)
    h_lo = (h - h_hi.astype(F32)).astype(BF16)
    w_lo = (w - w_hi.astype(F32)).astype(BF16)
    logits = (jnp.dot(h_hi, w_hi, preferred_element_type=F32)
              + (jnp.dot(h_hi, w_lo, preferred_element_type=F32)
                 + jnp.dot(h_lo, w_hi, preferred_element_type=F32)))
    idx = lax.broadcasted_iota(I32, (tr, n_exp), 1).astype(F32)
    m1 = jnp.max(logits, axis=-1, keepdims=True)
    i1 = jnp.min(jnp.where(logits == m1, idx, float(n_exp)), axis=-1, keepdims=True)
    rest = jnp.where(idx == i1, -jnp.inf, logits)
    m2 = jnp.max(rest, axis=-1, keepdims=True)
    i2 = jnp.min(jnp.where(rest == m2, idx, float(n_exp)), axis=-1, keepdims=True)
    e2 = jnp.exp(m2 - m1)
    den = 1.0 + e2
    oh1 = (idx == i1).astype(F32)
    oh2 = (idx == i2).astype(F32)
    both = oh1 + oh2
    r = lax.broadcasted_iota(I32, (tr, tr), 0)
    c = lax.broadcasted_iota(I32, (tr, tr), 1)
    tri = jnp.where(c < r, 1.0, 0.0).astype(BF16)
    before = jnp.dot(tri, both.astype(BF16), preferred_element_type=F32) + carry_ref[...]
    rank1 = jnp.sum(oh1 * before, axis=-1, keepdims=True)
    rank2 = jnp.sum(oh2 * before, axis=-1, keepdims=True)
    carry_ref[...] += jnp.sum(both, axis=0, keepdims=True)
    k = lax.broadcasted_iota(I32, (tr, 2), 1)
    eid_ref[...] = jnp.where(k == 0, i1, i2).astype(I32)
    rank_ref[...] = jnp.where(k == 0, rank1, rank2).astype(I32)
    gate_ref[...] = jnp.where(k == 0, 1.0 / den, e2 / den)
    cnt_ref[...] = carry_ref[...].astype(I32)


def _router(xa, xb, g, w_router, tr):
    d = xa.shape[1]
    na, nb = xa.shape[0] // tr, xb.shape[0] // tr
    n_exp = w_router.shape[1]
    rows = xa.shape[0] + xb.shape[0]
    amap = lambda i: (jnp.minimum(i, na - 1), 0)
    bmap = lambda i: (jnp.maximum(i - na, 0), 0)
    pair = lambda dt: jax.ShapeDtypeStruct((rows, 2), dt)
    return pl.pallas_call(
        functools.partial(_router_kernel, na=na),
        grid=(na + nb,),
        in_specs=[pl.BlockSpec((tr, d), amap), pl.BlockSpec((tr, d), bmap),
                  pl.BlockSpec((1, d), lambda i: (0, 0)),
                  pl.BlockSpec((d, n_exp), lambda i: (0, 0))],
        out_specs=[pl.BlockSpec((tr, 2), lambda i: (i, 0))] * 3
                  + [pl.BlockSpec((1, n_exp), lambda i: (0, 0))],
        out_shape=[pair(I32), pair(I32), pair(F32), jax.ShapeDtypeStruct((1, n_exp), I32)],
        scratch_shapes=[pltpu.VMEM((1, n_exp), F32)],
        compiler_params=_params(("arbitrary",)),
        name="moe_router",
    )(xa, xb, g.reshape(1, d), w_router)


DMA_UNROLL = 8


def _dispatch_kernel(fill_ref, pos_ref, xa_ref, xb_ref, o_ref, zero_ref, sem, zsem, *, na, td, n_exp):
    i = pl.program_id(0)
    tm = zero_ref.shape[0]

    def run(x_ref):
        def copy(row, dst):
            return pltpu.make_async_copy(x_ref.at[pl.ds(row, 1)], o_ref.at[pl.ds(dst, 1)], sem)

        def issue(g, carry):
            for u in range(DMA_UNROLL):
                r = g * DMA_UNROLL + u
                copy(r, pos_ref[0, 0, 2 * r]).start(priority=0)
                copy(r, pos_ref[0, 0, 2 * r + 1]).start(priority=1)
            return carry
        lax.fori_loop(0, td // DMA_UNROLL, issue, 0)

        def drain(g, carry):
            for _ in range(2 * DMA_UNROLL):
                copy(0, 0).wait()
            return carry
        lax.fori_loop(0, td // DMA_UNROLL, drain, 0)

        @pl.when(i == pl.num_programs(0) - 1)
        def _():
            for e in range(n_exp):
                lo, hi = fill_ref[2 * e], fill_ref[2 * e + 1]

                def fill(r, carry):
                    copy(0, r).start()
                    return carry
                lax.fori_loop(lo, hi, fill, 0)

                def drain_one(r, carry):
                    copy(0, 0).wait()
                    return carry
                lax.fori_loop(lo, hi, drain_one, 0)

    @pl.when(i < na)
    def _():
        run(xa_ref)

    @pl.when(i >= na)
    def _():
        run(xb_ref)

    @pl.when(i == pl.num_programs(0) - 1)
    def _():
        zero_ref[...] = jnp.zeros_like(zero_ref)

        def zero_tile(t, carry):
            cp = pltpu.make_async_copy(zero_ref, o_ref.at[pl.ds(pl.multiple_of(t * tm, tm), tm)], zsem)
            cp.start()
            cp.wait()
            return carry
        lax.fori_loop(fill_ref[2 * n_exp], o_ref.shape[0] // tm, zero_tile, 0)


def _dispatch(xa, xb, pos, fill, rows_out, td, tm, n_exp):
    d = xa.shape[1]
    na, nb = xa.shape[0] // td, xb.shape[0] // td
    grid_spec = pltpu.PrefetchScalarGridSpec(
        num_scalar_prefetch=1,
        grid=(na + nb,),
        in_specs=[pl.BlockSpec((1, 1, 2 * td), lambda i, f: (i, 0, 0), memory_space=pltpu.SMEM),
                  pl.BlockSpec((td, d), lambda i, f: (jnp.minimum(i, na - 1), 0)),
                  pl.BlockSpec((td, d), lambda i, f: (jnp.maximum(i - na, 0), 0))],
        out_specs=pl.BlockSpec(memory_space=pl.ANY),
        scratch_shapes=[pltpu.VMEM((tm, d), F32), pltpu.SemaphoreType.DMA, pltpu.SemaphoreType.DMA],
    )
    return pl.pallas_call(
        functools.partial(_dispatch_kernel, na=na, td=td, n_exp=n_exp),
        grid_spec=grid_spec,
        out_shape=jax.ShapeDtypeStruct((rows_out, d), F32),
        compiler_params=_params(("arbitrary",)),
        name="moe_dispatch",
    )(fill, pos.reshape(na + nb, 1, 2 * td), xa, xb)


def _combine_kernel(pos_ref, posn_ref, xa_ref, xb_ref, gate_ref, g_ref, y_ref, oa_ref, ob_ref,
                    buf_ref, sem, *, na):
    i = pl.program_id(0)
    tc = xa_ref.shape[0]
    slot = lax.rem(i, 2)

    def copy(s, k, r, src):
        return pltpu.make_async_copy(y_ref.at[pl.ds(src, 1)], buf_ref.at[s, k, pl.ds(r, 1)],
                                     sem.at[s])

    def issue(p_ref, s):
        def body(g, carry):
            for u in range(DMA_UNROLL):
                r = g * DMA_UNROLL + u
                copy(s, 0, r, p_ref[0, 0, 2 * r]).start(priority=0)
                copy(s, 1, r, p_ref[0, 0, 2 * r + 1]).start(priority=1)
            return carry
        lax.fori_loop(0, tc // DMA_UNROLL, body, 0)

    @pl.when(i == 0)
    def _():
        issue(pos_ref, 0)

    @pl.when(i + 1 < pl.num_programs(0))
    def _():
        issue(posn_ref, 1 - slot)

    def drain(g, carry):
        for _ in range(2 * DMA_UNROLL):
            copy(slot, 0, 0, 0).wait()
        return carry
    lax.fori_loop(0, tc // DMA_UNROLL, drain, 0)

    x = _two_source(i, na, xa_ref, xb_ref)
    gate = gate_ref[...]
    out = _rms(x + (gate[:, 0:1] * buf_ref[slot, 0] + gate[:, 1:2] * buf_ref[slot, 1]), g_ref[...])

    @pl.when(i < na)
    def _():
        oa_ref[...] = out

    @pl.when(i >= na)
    def _():
        ob_ref[...] = out


def _combine(xa, xb, gates, pos, y_sorted, g_final, tc):
    d = xa.shape[1]
    na, nb = xa.shape[0] // tc, xb.shape[0] // tc
    n = na + nb
    amap = lambda i: (jnp.minimum(i, na - 1), 0)
    bmap = lambda i: (jnp.maximum(i - na, 0), 0)
    pos = pos.reshape(n, 1, 2 * tc)
    return pl.pallas_call(
        functools.partial(_combine_kernel, na=na),
        grid=(n,),
        in_specs=[pl.BlockSpec((1, 1, 2 * tc), lambda i: (i, 0, 0), memory_space=pltpu.SMEM),
                  pl.BlockSpec((1, 1, 2 * tc), lambda i: (jnp.minimum(i + 1, n - 1), 0, 0),
                               memory_space=pltpu.SMEM),
                  pl.BlockSpec((tc, d), amap), pl.BlockSpec((tc, d), bmap),
                  pl.BlockSpec((tc, 2), lambda i: (i, 0)),
                  pl.BlockSpec((1, d), lambda i: (0, 0)),
                  pl.BlockSpec(memory_space=pl.ANY)],
        out_specs=[pl.BlockSpec((tc, d), amap), pl.BlockSpec((tc, d), bmap)],
        out_shape=[jax.ShapeDtypeStruct(xa.shape, F32), jax.ShapeDtypeStruct(xb.shape, F32)],
        scratch_shapes=[pltpu.VMEM((2, 2, tc, d), F32), pltpu.SemaphoreType.DMA((2,))],
        compiler_params=_params(("arbitrary",)),
        name="moe_combine",
    )(pos, pos, xa, xb, gates, g_final.reshape(1, d), y_sorted)


def _moe(xa, xb, g_ffn, w_router, w_gu, w_down, g_final, tm, tf):
    n_exp = w_router.shape[1]
    rows = xa.shape[0] + xb.shape[0]
    eid, rank, gates, counts = _router(xa, xb, g_ffn, w_router, tr=512)
    counts = counts[0]
    padded = (counts + tm - 1) // tm * tm
    ends = jnp.cumsum(padded)
    offs = ends - padded
    pos = (offs[eid] + rank).reshape(-1)
    n_tiles = (2 * rows + n_exp * (tm - 1)) // tm
    n_used = (ends[-1] // tm).astype(I32)
    tile_start = jnp.arange(n_tiles, dtype=I32) * tm
    tile_expert = jnp.sum(tile_start[:, None] >= ends[None, :], axis=1).astype(I32)
    last = jnp.take(tile_expert, jnp.maximum(n_used - 1, 0))
    tile_expert = jnp.minimum(jnp.where(tile_start < ends[-1], tile_expert, last), n_exp - 1)
    fill = jnp.concatenate([jnp.stack([offs + counts, offs + padded], axis=1).reshape(-1),
                            n_used.reshape(1)]).astype(I32)
    x_sorted = _dispatch(xa, xb, pos, fill, n_tiles * tm, td=512, tm=tm, n_exp=n_exp)
    y_sorted = _ffn(x_sorted, g_ffn, w_gu, w_down, tile_expert, n_used.reshape(1), tm, tf,
                    residual=False)
    return _combine(xa, xb, gates, pos, y_sorted, g_final, tc=256)


def kernel(x_prompt, x_sample, mem_prompt, cache_swa_k, cache_swa_v, state_ssm_re, state_ssm_im, cache_mem_k, cache_mem_v, g_mix, g_ffn, g_mem, g_final, w_mem_kv, w_in_a, sinks, w_out_a, w_in_b, lam_re, lam_im, log_step, b_re, b_im, c_re, c_im, d_skip, w_glu, b_glu, w_out_b, w_ffn_gu, w_ffn_down, w_router, w_exp_gu, w_exp_down):
    n_batch, seq, d = x_prompt.shape
    n_dec, t_new, _ = x_sample.shape
    n_mem = mem_prompt.shape[1]
    depth = g_mix.shape[0]
    assert depth == 2 and w_in_a.shape[0] == 1 and w_in_b.shape[0] == 1
    n_kv = cache_swa_k.shape[3]
    n_q = sinks.shape[1]
    group = n_q // n_kv
    n_x = cache_mem_k.shape[3]
    x_q = n_x * HEAD_DIM
    kv_a = n_kv * HEAD_DIM
    q_a = n_q * HEAD_DIM
    s5_width = w_glu.shape[1]
    n_groups, n_state_g = lam_re.shape[1:]
    n_state = n_groups * n_state_g
    cache_w = cache_swa_k.shape[2]

    xp = x_prompt.reshape(n_batch * seq, d)
    xs = x_sample.reshape(n_dec * t_new, d)
    mem = mem_prompt.reshape(n_batch * n_mem, d)
    bt = 16
    chunk = 1024
    sub_len = chunk // SUBSEQ

    memkv = [_proj(mem, g_mem[i], w_mem_kv[i], tm=n_mem) for i in range(depth)]
    mem_k_prompt = jnp.stack([m[:, :x_q].reshape(n_batch, n_mem, n_x, HEAD_DIM) for m in memkv])
    mem_v_prompt = jnp.stack([m[:, x_q:].reshape(n_batch, n_mem, n_x, HEAD_DIM) for m in memkv])

    w_in = jnp.concatenate(
        [w_in_a[0][:, :q_a].reshape(d, n_kv, group, HEAD_DIM).transpose(0, 2, 1, 3).reshape(d, q_a),
         w_in_a[0][:, q_a:]], axis=1)
    w_out = jnp.concatenate(
        [w_out_a[0][:q_a].reshape(n_kv, group, HEAD_DIM, d).transpose(1, 0, 2, 3).reshape(q_a, d),
         w_out_a[0][q_a:]], axis=0)
    sink = sinks[0].reshape(n_kv, group).T.reshape(-1)
    pa_p = _proj(xp, g_mix[0], w_in, tm=512)
    pa_s = _proj(xs, g_mix[0], w_in, tm=512)
    x1p = _mixa_prompt(pa_p, xp, memkv[0], sink, w_out, n_batch, n_kv, group, n_x, n_mem)
    x1s, swa_ks, swa_vs = _mixa_sample(
        pa_s, xs, cache_swa_k[0].reshape(n_dec, cache_w, kv_a), cache_swa_v[0].reshape(n_dec, cache_w, kv_a),
        cache_mem_k[0].reshape(n_dec, n_mem, x_q), cache_mem_v[0].reshape(n_dec, n_mem, x_q),
        sink, w_out, n_kv, group, n_x, t_new, bt)
    w_keep = min(WINDOW, seq)
    kv_p = pa_p.reshape(n_batch, seq, -1)[:, seq - w_keep:, q_a:q_a + 2 * kv_a]
    swa_k_prompt = kv_p[..., :kv_a].reshape(1, n_batch, w_keep, n_kv, HEAD_DIM)
    swa_v_prompt = kv_p[..., kv_a:].reshape(1, n_batch, w_keep, n_kv, HEAD_DIM)
    swa_k_sample = swa_ks.reshape(1, n_dec, cache_w, n_kv, HEAD_DIM)
    swa_v_sample = swa_vs.reshape(1, n_dec, cache_w, n_kv, HEAD_DIM)

    x2p = _ffn(x1p, g_ffn[0], w_ffn_gu, w_ffn_down, jnp.zeros((x1p.shape[0] // 1024,), I32),
               jnp.full((1,), x1p.shape[0] // 1024, I32), 1024, 512, residual=True)
    x2s = _ffn(x1s, g_ffn[0], w_ffn_gu, w_ffn_down, jnp.zeros((x1s.shape[0] // 1024,), I32),
               jnp.full((1,), x1s.shape[0] // 1024, I32), 1024, 512, residual=True)

    pb_p = _proj(x2p, g_mix[1], w_in_b[0], tm=512)
    pb_s = _proj(x2s, g_mix[1], w_in_b[0], tm=512)
    sw = _s5_weights(lam_re[0], lam_im[0], log_step[0], b_re[0], b_im[0], c_re[0], c_im[0], sub_len)
    nchunk = seq // chunk
    u_perm = (pb_p[:, :s5_width].astype(BF16)
              .reshape(n_batch, nchunk, SUBSEQ, sub_len, s5_width)
              .transpose(0, 1, 3, 2, 4).reshape(n_batch * seq, s5_width))
    so_perm, h_fin = _s5_prompt(u_perm, sw, d_skip[0], w_glu[0], b_glu[0], n_batch, chunk)
    so_p = (so_perm.reshape(n_batch, nchunk, sub_len, SUBSEQ, s5_width)
            .transpose(0, 1, 3, 2, 4).reshape(n_batch * seq, s5_width))
    ssm_re_prompt = h_fin[:, 0, :n_state].reshape(1, n_batch, n_groups, n_state_g)
    ssm_im_prompt = h_fin[:, 0, n_state:].reshape(1, n_batch, n_groups, n_state_g)

    u_tm = (pb_s[:, :s5_width].astype(BF16).reshape(n_dec, t_new, s5_width)
            .transpose(1, 0, 2).reshape(n_dec * t_new, s5_width))
    so_tm, hr_s, hi_s = _s5_sample(u_tm, state_ssm_re[0].reshape(n_dec, n_state),
                                   state_ssm_im[0].reshape(n_dec, n_state), sw,
                                   d_skip[0], w_glu[0], b_glu[0], t_new)
    so_s = so_tm.reshape(t_new, n_dec, s5_width).transpose(1, 0, 2).reshape(n_dec * t_new, s5_width)
    ssm_re_sample = hr_s.reshape(1, n_dec, n_groups, n_state_g)
    ssm_im_sample = hi_s.reshape(1, n_dec, n_groups, n_state_g)

    x3p = _tailb_prompt(so_p, pb_p, x2p, memkv[1], w_out_b[0], n_batch, n_x, n_mem, tm=512)
    x3s = _tailb_sample(so_s, pb_s, x2s, cache_mem_k[1].reshape(n_dec, n_mem, x_q),
                        cache_mem_v[1].reshape(n_dec, n_mem, x_q), w_out_b[0], n_x, t_new, bt)

    yp, ys = _moe(x3p, x3s, g_ffn[1], w_router[0], w_exp_gu[0], w_exp_down[0], g_final, 1024, 512)
    return (yp.reshape(n_batch, seq, d), ys.reshape(n_dec, t_new, d),
            swa_k_prompt, swa_v_prompt, ssm_re_prompt, ssm_im_prompt,
            mem_k_prompt, mem_v_prompt, swa_k_sample, swa_v_sample,
            ssm_re_sample, ssm_im_sample)
```

```python
import functools

import jax
import jax.numpy as jnp
from jax import lax
from jax.experimental import pallas as pl
from jax.experimental.pallas import tpu as pltpu

F32 = jnp.float32
BF16 = jnp.bfloat16
I32 = jnp.int32

HEAD_DIM = 64
WINDOW = 128
BLOCK = 128
EPS = 1e-5
SCALE = HEAD_DIM ** -0.5
S5_GROUP = 16
SUBSEQ = 8
SUBLANES = 8
LANE_WIN = 512
VMEM_LIMIT = 56 * 1024 * 1024

_NT = (((1,), (1,)), ((), ()))


def _params(sem, vmem=VMEM_LIMIT):
    return pltpu.CompilerParams(dimension_semantics=sem, vmem_limit_bytes=vmem)


def _rms(x, g):
    return x * lax.rsqrt(jnp.mean(x * x, axis=-1, keepdims=True) + EPS) * g


def _softmax(s):
    m = jnp.max(s, axis=-1, keepdims=True)
    p = jnp.exp(s - m)
    return p / jnp.sum(p, axis=-1, keepdims=True)


def _proj_kernel(x_ref, g_ref, w_ref, o_ref):
    h = _rms(x_ref[...], g_ref[...]).astype(BF16)
    o_ref[...] = jnp.dot(h, w_ref[...], preferred_element_type=F32)


def _proj(x, g, w, tm):
    rows, d = x.shape
    n = w.shape[1]
    return pl.pallas_call(
        _proj_kernel,
        grid=(rows // tm,),
        in_specs=[pl.BlockSpec((tm, d), lambda i: (i, 0)),
                  pl.BlockSpec((1, d), lambda i: (0, 0)),
                  pl.BlockSpec((d, n), lambda i: (0, 0))],
        out_specs=pl.BlockSpec((tm, n), lambda i: (i, 0)),
        out_shape=jax.ShapeDtypeStruct((rows, n), F32),
        compiler_params=_params(("parallel",)),
        name="rms_proj",
    )(x, g.reshape(1, d), w.astype(BF16))


def _sink_attention_group(q_rows, k_h, v_h, mask, sink_col):
    s = lax.dot_general(q_rows, k_h, _NT, preferred_element_type=F32)
    s = jnp.where(mask, s, -jnp.inf)
    m = jnp.maximum(jnp.max(s, axis=-1, keepdims=True), sink_col)
    p = jnp.exp(s - m)
    den = jnp.sum(p, axis=-1, keepdims=True) + jnp.exp(sink_col - m)
    return jnp.dot((p / den).astype(BF16), v_h, preferred_element_type=F32)


def _cross_attention(xq, mk, mv, n_heads, mix_ref, rows, col0):
    xq = (xq * SCALE).astype(BF16)
    for h in range(n_heads):
        sl = slice(h * HEAD_DIM, (h + 1) * HEAD_DIM)
        s = lax.dot_general(xq[:, sl], mk[:, sl], _NT, preferred_element_type=F32)
        o = jnp.dot(_softmax(s).astype(BF16), mv[:, sl], preferred_element_type=F32)
        mix_ref[rows, col0 + h * HEAD_DIM:col0 + (h + 1) * HEAD_DIM] = o


def _sink_column(sink_ref, heads, rows_per_head):
    n = len(heads)
    row = lax.broadcasted_iota(I32, (n * rows_per_head, 1), 0)
    col = jnp.full((n * rows_per_head, 1), sink_ref[0, heads[n - 1]], F32)
    for g in range(n - 2, -1, -1):
        col = jnp.where(row < (g + 1) * rows_per_head, sink_ref[0, heads[g]], col)
    return col


def _head_mask(n_heads, t):
    shape = (n_heads * t, n_heads * HEAD_DIM)
    row = lax.broadcasted_iota(I32, shape, 0)
    col = lax.broadcasted_iota(I32, shape, 1)
    lo = (row // t) * HEAD_DIM
    return (col >= lo) & (col < lo + HEAD_DIM)


def _spread_heads(x, hmask, n_heads):
    return jnp.where(hmask, jnp.concatenate([x] * n_heads, axis=0), 0.0).astype(BF16)


def _gather_heads(o, hmask, n_heads, t):
    o = jnp.where(hmask, o, 0.0)
    acc = o[0:t]
    for h in range(1, n_heads):
        acc = acc + o[h * t:(h + 1) * t]
    return acc


def _cross_attention_small(xq, mk, mv, hmask, n_heads):
    t = xq.shape[0]
    s = lax.dot_general(_spread_heads(xq * SCALE, hmask, n_heads), mk, _NT,
                        preferred_element_type=F32)
    o = jnp.dot(_softmax(s).astype(BF16), mv, preferred_element_type=F32)
    return _gather_heads(o, hmask, n_heads, t)


def _out_proj(x_ref, mix_ref, wo_ref, o_ref):
    o_ref[...] = x_ref[...] + jnp.dot(mix_ref[...].astype(BF16), wo_ref[...],
                                      preferred_element_type=F32)


def _mixa_prompt_kernel(sink_ref, q_ref, kp_ref, kc_ref, vp_ref, vc_ref, xq_ref, x_ref,
                        mk_ref, mv_ref, wo_ref, o_ref, mix_ref, *, n_kv, group, n_x):
    i = pl.program_id(1)
    q = q_ref[...] * SCALE
    m_rows = group * BLOCK
    qi = lax.rem(lax.broadcasted_iota(I32, (m_rows, 2 * BLOCK), 0), BLOCK)
    kj = lax.broadcasted_iota(I32, (m_rows, 2 * BLOCK), 1)
    dist = qi + BLOCK - kj
    band = (dist >= 0) & (dist <= WINDOW)
    first = jnp.where(i > 0, 0, BLOCK)
    for sb in range(q_ref.shape[0] // BLOCK):
        rows = slice(sb * BLOCK, (sb + 1) * BLOCK)
        before = slice((sb - 1) * BLOCK, sb * BLOCK)
        k_prev = kp_ref[...] if sb == 0 else kc_ref[before, :]
        v_prev = vp_ref[...] if sb == 0 else vc_ref[before, :]
        kk = jnp.concatenate([k_prev, kc_ref[rows, :]], axis=0).astype(BF16)
        vv = jnp.concatenate([v_prev, vc_ref[rows, :]], axis=0).astype(BF16)
        mask = (band & (kj >= first)) if sb == 0 else band
        for h in range(n_kv):
            sl = slice(h * HEAD_DIM, (h + 1) * HEAD_DIM)
            cols = [(g * n_kv + h) * HEAD_DIM for g in range(group)]
            qh = jnp.concatenate([q[rows, c0:c0 + HEAD_DIM] for c0 in cols], axis=0).astype(BF16)
            o = _sink_attention_group(
                qh, kk[:, sl], vv[:, sl], mask,
                _sink_column(sink_ref, [g * n_kv + h for g in range(group)], BLOCK))
            for g, c0 in enumerate(cols):
                mix_ref[rows, c0:c0 + HEAD_DIM] = o[g * BLOCK:(g + 1) * BLOCK]
    q_a = n_kv * group * HEAD_DIM
    _cross_attention(xq_ref[...], mk_ref[...].astype(BF16), mv_ref[...].astype(BF16), n_x,
                     mix_ref, slice(None), q_a)
    _out_proj(x_ref, mix_ref, wo_ref, o_ref)


def _mixa_prompt(p, x, memkv, sinks, w_out, n_batch, n_kv, group, n_x, n_mem):
    rows, d = x.shape
    nb = rows // n_batch // BLOCK
    q_a = n_kv * group * HEAD_DIM
    kv_a = n_kv * HEAD_DIM
    x_q = n_x * HEAD_DIM
    kcol = q_a // kv_a
    vcol = (q_a + kv_a) // kv_a
    xcol = (q_a + 2 * kv_a) // x_q
    qt = 2
    tm = qt * BLOCK
    nt = nb // qt
    cur = lambda b, i: b * nt + i
    prev = lambda b, i: b * nb + jnp.maximum(i * qt - 1, 0)
    kern = functools.partial(_mixa_prompt_kernel, n_kv=n_kv, group=group, n_x=n_x)
    return pl.pallas_call(
        kern,
        grid=(n_batch, nt),
        in_specs=[
            pl.BlockSpec(memory_space=pltpu.SMEM),
            pl.BlockSpec((tm, q_a), lambda b, i: (cur(b, i), 0)),
            pl.BlockSpec((BLOCK, kv_a), lambda b, i: (prev(b, i), kcol)),
            pl.BlockSpec((tm, kv_a), lambda b, i: (cur(b, i), kcol)),
            pl.BlockSpec((BLOCK, kv_a), lambda b, i: (prev(b, i), vcol)),
            pl.BlockSpec((tm, kv_a), lambda b, i: (cur(b, i), vcol)),
            pl.BlockSpec((tm, x_q), lambda b, i: (cur(b, i), xcol)),
            pl.BlockSpec((tm, d), lambda b, i: (cur(b, i), 0)),
            pl.BlockSpec((n_mem, x_q), lambda b, i: (b, 0)),
            pl.BlockSpec((n_mem, x_q), lambda b, i: (b, 1)),
            pl.BlockSpec((q_a + x_q, d), lambda b, i: (0, 0)),
        ],
        out_specs=pl.BlockSpec((tm, d), lambda b, i: (cur(b, i), 0)),
        out_shape=jax.ShapeDtypeStruct((rows, d), F32),
        scratch_shapes=[pltpu.VMEM((tm, q_a + x_q), F32)],
        compiler_params=_params(("parallel", "parallel")),
        name="mix_a_prompt",
    )(sinks.reshape(1, -1), p, p, p, p, p, p, x, memkv, memkv, w_out.astype(BF16))


def _mixa_sample_kernel(sink_ref, p_ref, x_ref, ck_ref, cv_ref, mk_ref, mv_ref, wo_ref,
                        o_ref, ok_ref, ov_ref, mix_ref, *, n_kv, group, n_x, t_new, bt):
    w = ck_ref.shape[1]
    q_a = n_kv * group * HEAD_DIM
    kv_a = n_kv * HEAD_DIM
    m_rows = group * n_kv * t_new
    ti = lax.rem(lax.broadcasted_iota(I32, (m_rows, w + t_new), 0), t_new)
    kj = lax.broadcasted_iota(I32, (m_rows, w + t_new), 1)
    dist = ti + w - kj
    mask = (dist >= 0) & (dist <= WINDOW)
    kv_mask = _head_mask(n_kv, t_new)
    x_mask = _head_mask(n_x, t_new)
    sink_col = sink_ref[...]

    def body(bb, carry):
        rows = pl.ds(pl.multiple_of(bb * t_new, t_new), t_new)
        q = p_ref[rows, 0:q_a] * SCALE
        k_new = p_ref[rows, q_a:q_a + kv_a]
        v_new = p_ref[rows, q_a + kv_a:q_a + 2 * kv_a]
        xq = p_ref[rows, q_a + 2 * kv_a:]
        k_old = ck_ref[bb]
        v_old = cv_ref[bb]
        ok_ref[bb, 0:w - t_new, :] = k_old[t_new:, :]
        ok_ref[bb, w - t_new:w, :] = k_new
        ov_ref[bb, 0:w - t_new, :] = v_old[t_new:, :]
        ov_ref[bb, w - t_new:w, :] = v_new
        kk = jnp.concatenate([k_old, k_new], axis=0).astype(BF16)
        vv = jnp.concatenate([v_old, v_new], axis=0).astype(BF16)
        qs = jnp.concatenate([_spread_heads(q[:, g * kv_a:(g + 1) * kv_a], kv_mask, n_kv)
                              for g in range(group)], axis=0)
        o = _sink_attention_group(qs, kk, vv, mask, sink_col)
        blk = n_kv * t_new
        for g in range(group):
            mix_ref[rows, g * kv_a:(g + 1) * kv_a] = _gather_heads(
                o[g * blk:(g + 1) * blk], kv_mask, n_kv, t_new)
        mix_ref[rows, q_a:] = _cross_attention_small(
            xq, mk_ref[bb].astype(BF16), mv_ref[bb].astype(BF16), x_mask, n_x)
        return carry

    lax.fori_loop(0, bt, body, 0, unroll=2)
    _out_proj(x_ref, mix_ref, wo_ref, o_ref)


def _mixa_sample(p, x, cache_k, cache_v, mem_k, mem_v, sinks, w_out, n_kv, group, n_x, t_new, bt):
    rows, d = x.shape
    n_dec, w, kv_a = cache_k.shape
    n_mem, x_q = mem_k.shape[1:]
    q_a = n_kv * group * HEAD_DIM
    tm = bt * t_new
    kern = functools.partial(_mixa_sample_kernel, n_kv=n_kv, group=group, n_x=n_x,
                             t_new=t_new, bt=bt)
    return pl.pallas_call(
        kern,
        grid=(n_dec // bt,),
        in_specs=[
            pl.BlockSpec((n_kv * group * t_new, 1), lambda i: (0, 0)),
            pl.BlockSpec((tm, p.shape[1]), lambda i: (i, 0)),
            pl.BlockSpec((tm, d), lambda i: (i, 0)),
            pl.BlockSpec((bt, w, kv_a), lambda i: (i, 0, 0)),
            pl.BlockSpec((bt, w, kv_a), lambda i: (i, 0, 0)),
            pl.BlockSpec((bt, n_mem, x_q), lambda i: (i, 0, 0)),
            pl.BlockSpec((bt, n_mem, x_q), lambda i: (i, 0, 0)),
            pl.BlockSpec((q_a + x_q, d), lambda i: (0, 0)),
        ],
        out_specs=[pl.BlockSpec((tm, d), lambda i: (i, 0)),
                   pl.BlockSpec((bt, w, kv_a), lambda i: (i, 0, 0)),
                   pl.BlockSpec((bt, w, kv_a), lambda i: (i, 0, 0))],
        out_shape=[jax.ShapeDtypeStruct((rows, d), F32),
                   jax.ShapeDtypeStruct((n_dec, w, kv_a), F32),
                   jax.ShapeDtypeStruct((n_dec, w, kv_a), F32)],
        scratch_shapes=[pltpu.VMEM((tm, q_a + x_q), F32)],
        compiler_params=_params(("parallel",)),
        name="mix_a_sample",
    )(jnp.repeat(sinks, t_new).reshape(-1, 1), p, x, cache_k, cache_v, mem_k, mem_v,
      w_out.astype(BF16))


def _tailb_prompt_kernel(s_ref, xq_ref, x_ref, mk_ref, mv_ref, wo_ref, o_ref, mix_ref, *, n_x):
    width = s_ref.shape[1]
    mix_ref[:, 0:width] = s_ref[...].astype(F32)
    _cross_attention(xq_ref[...], mk_ref[...].astype(BF16), mv_ref[...].astype(BF16), n_x,
                     mix_ref, slice(None), width)
    _out_proj(x_ref, mix_ref, wo_ref, o_ref)


def _tailb_prompt(s_out, p, x, memkv, w_out, n_batch, n_x, n_mem, tm):
    rows, d = x.shape
    width = s_out.shape[1]
    x_q = n_x * HEAD_DIM
    per_b = rows // n_batch // tm
    kern = functools.partial(_tailb_prompt_kernel, n_x=n_x)
    return pl.pallas_call(
        kern,
        grid=(rows // tm,),
        in_specs=[
            pl.BlockSpec((tm, width), lambda i: (i, 0)),
            pl.BlockSpec((tm, x_q), lambda i: (i, width // x_q)),
            pl.BlockSpec((tm, d), lambda i: (i, 0)),
            pl.BlockSpec((n_mem, x_q), lambda i: (i // per_b, 0)),
            pl.BlockSpec((n_mem, x_q), lambda i: (i // per_b, 1)),
            pl.BlockSpec((width + x_q, d), lambda i: (0, 0)),
        ],
        out_specs=pl.BlockSpec((tm, d), lambda i: (i, 0)),
        out_shape=jax.ShapeDtypeStruct((rows, d), F32),
        scratch_shapes=[pltpu.VMEM((tm, width + x_q), F32)],
        compiler_params=_params(("parallel",)),
        name="tail_b_prompt",
    )(s_out, p, x, memkv, memkv, w_out.astype(BF16))


def _tailb_sample_kernel(s_ref, xq_ref, x_ref, mk_ref, mv_ref, wo_ref, o_ref, mix_ref,
                         *, n_x, t_new, bt):
    width = s_ref.shape[1]
    mix_ref[:, 0:width] = s_ref[...].astype(F32)

    x_mask = _head_mask(n_x, t_new)

    def body(bb, carry):
        rows = pl.ds(pl.multiple_of(bb * t_new, t_new), t_new)
        mix_ref[rows, width:] = _cross_attention_small(
            xq_ref[rows, :], mk_ref[bb].astype(BF16), mv_ref[bb].astype(BF16), x_mask, n_x)
        return carry

    lax.fori_loop(0, bt, body, 0, unroll=2)
    _out_proj(x_ref, mix_ref, wo_ref, o_ref)


def _tailb_sample(s_out, p, x, mem_k, mem_v, w_out, n_x, t_new, bt):
    rows, d = x.shape
    width = s_out.shape[1]
    n_dec, n_mem, x_q = mem_k.shape
    tm = bt * t_new
    kern = functools.partial(_tailb_sample_kernel, n_x=n_x, t_new=t_new, bt=bt)
    return pl.pallas_call(
        kern,
        grid=(n_dec // bt,),
        in_specs=[
            pl.BlockSpec((tm, width), lambda i: (i, 0)),
            pl.BlockSpec((tm, x_q), lambda i: (i, width // x_q)),
            pl.BlockSpec((tm, d), lambda i: (i, 0)),
            pl.BlockSpec((bt, n_mem, x_q), lambda i: (i, 0, 0)),
            pl.BlockSpec((bt, n_mem, x_q), lambda i: (i, 0, 0)),
            pl.BlockSpec((width + x_q, d), lambda i: (0, 0)),
        ],
        out_specs=pl.BlockSpec((tm, d), lambda i: (i, 0)),
        out_shape=jax.ShapeDtypeStruct((rows, d), F32),
        scratch_shapes=[pltpu.VMEM((tm, width + x_q), F32)],
        compiler_params=_params(("parallel",)),
        name="tail_b_sample",
    )(s_out, p, x, mem_k, mem_v, w_out.astype(BF16))


def _s5_disc_kernel(lr_ref, li_ref, ls_ref, br_ref, bi_ref,
                    ar_ref, ai_ref, apr_ref, api_ref, bbr_ref, bbi_ref, *, n_square):
    lr, li = lr_ref[...], li_ref[...]
    step = jnp.exp(ls_ref[...])
    mag = jnp.exp(lr * step)
    ar, ai = mag * jnp.cos(li * step), mag * jnp.sin(li * step)
    den = lr * lr + li * li
    cr = ((ar - 1.0) * lr + ai * li) / den
    ci = (ai * lr - (ar - 1.0) * li) / den
    br, bi = br_ref[...], bi_ref[...]
    bbr_ref[...] = cr * br - ci * bi
    bbi_ref[...] = cr * bi + ci * br
    ar_ref[...] = ar
    ai_ref[...] = ai
    pr, pi = ar, ai
    for _ in range(n_square):
        pr, pi = pr * pr - pi * pi, 2.0 * pr * pi
    apr_ref[...] = pr
    api_ref[...] = pi


def _s5_weights(lam_re, lam_im, log_step, b_re, b_im, c_re, c_im, sub_len):
    g, p = lam_re.shape
    hh = b_re.shape[2]
    rep = lambda a: jnp.repeat(a, hh, axis=0)
    flat = lambda a: a.transpose(0, 2, 1).reshape(g * hh, p)
    n_square = sub_len.bit_length() - 1
    assert 1 << n_square == sub_len
    shp = jax.ShapeDtypeStruct((g * hh, p), F32)
    ar, ai, apr, api, bbr, bbi = pl.pallas_call(
        functools.partial(_s5_disc_kernel, n_square=n_square),
        out_shape=[shp] * 6,
        name="s5_discretise",
    )(rep(lam_re), rep(lam_im), rep(jnp.broadcast_to(log_step[:, None], (g, p))),
      flat(b_re), flat(b_im))
    row = lambda a: a[::hh].reshape(1, g * p)
    gb = 256 // hh
    nblk = g // gb
    eye = jnp.eye(gb, dtype=F32)
    bd_in = lambda a: (a.reshape(nblk, gb, hh, 1, p) * eye[None, :, None, :, None]
                       ).reshape(nblk, gb * hh, gb * p)
    wb = jnp.concatenate([bd_in(bbr), bd_in(bbi)], axis=-1).astype(BF16)
    bd_out = lambda c: (c.reshape(nblk, gb, hh, p).transpose(0, 1, 3, 2)[:, :, :, None, :]
                        * eye[None, :, None, :, None]).reshape(nblk, gb * p, gb * hh)
    wc = jnp.concatenate([bd_out(c_re), -bd_out(c_im)], axis=1).astype(BF16)
    return row(ar), row(ai), row(apr), row(api), wb, wc


def _cstep(ar, ai, hr, hi, xr, xi):
    return ar * hr - ai * hi + xr, ar * hi + ai * hr + xi


def _glu_out(y, u, d_ref, wg_ref, bg_ref):
    z = jax.nn.gelu(y + d_ref[...] * u)
    gate = jnp.dot(z.astype(BF16), wg_ref[...], preferred_element_type=F32) + bg_ref[...]
    return z * jax.nn.sigmoid(gate)


def _s5_prompt_kernel(u_ref, ar_ref, ai_ref, apr_ref, api_ref, wb_ref, wc_ref, d_ref, wg_ref,
                      bg_ref, o_ref, hf_ref, xs_ref, hs_ref, y_ref, carry_ref, *, n_steps):
    c = pl.program_id(1)
    nblk, ublk, two_cw = wb_ref.shape
    cw = two_cw // 2
    n_state = nblk * cw

    @pl.when(c == 0)
    def _():
        carry_ref[...] = jnp.zeros_like(carry_ref)

    row8 = lax.broadcasted_iota(I32, (SUBLANES, LANE_WIN), 0)
    for j in range(nblk):
        xs_ref[...] = jnp.dot(u_ref[:, j * ublk:(j + 1) * ublk], wb_ref[j],
                              preferred_element_type=F32)
        for lb in range(cw // LANE_WIN):
            re = slice(lb * LANE_WIN, (lb + 1) * LANE_WIN)
            im = slice(cw + lb * LANE_WIN, cw + (lb + 1) * LANE_WIN)
            gre = slice(j * cw + lb * LANE_WIN, j * cw + (lb + 1) * LANE_WIN)
            gim = slice(n_state + j * cw + lb * LANE_WIN, n_state + j * cw + (lb + 1) * LANE_WIN)
            bc = lambda r: jnp.broadcast_to(r[:, gre], (SUBLANES, LANE_WIN))
            ar, ai, apr, api = bc(ar_ref), bc(ai_ref), bc(apr_ref), bc(api_ref)
            c_r = jnp.broadcast_to(carry_ref[:, gre], (SUBLANES, LANE_WIN))
            c_i = jnp.broadcast_to(carry_ref[:, gim], (SUBLANES, LANE_WIN))

            def rows(i):
                return pl.ds(pl.multiple_of(i * SUBLANES, SUBLANES), SUBLANES)

            def pass1(i, h):
                return _cstep(ar, ai, h[0], h[1], xs_ref[rows(i), re], xs_ref[rows(i), im])

            zero = jnp.zeros((SUBLANES, LANE_WIN), F32)
            e_r, e_i = lax.fori_loop(0, n_steps, pass1, (zero, zero), unroll=8)
            s_r = jnp.where(row8 == 0, c_r, 0.0)
            s_i = jnp.where(row8 == 0, c_i, 0.0)
            for _ in range(SUBSEQ - 1):
                n_r, n_i = _cstep(apr, api, s_r, s_i, e_r, e_i)
                s_r = jnp.where(row8 == 0, c_r, pltpu.roll(n_r, 1, 0))
                s_i = jnp.where(row8 == 0, c_i, pltpu.roll(n_i, 1, 0))
            n_r, n_i = _cstep(apr, api, s_r, s_i, e_r, e_i)
            carry_ref[:, gre] = n_r[SUBSEQ - 1:SUBSEQ, :]
            carry_ref[:, gim] = n_i[SUBSEQ - 1:SUBSEQ, :]

            def pass2(i, h):
                h = _cstep(ar, ai, h[0], h[1], xs_ref[rows(i), re], xs_ref[rows(i), im])
                hs_ref[rows(i), re] = h[0]
                hs_ref[rows(i), im] = h[1]
                return h

            lax.fori_loop(0, n_steps, pass2, (s_r, s_i), unroll=4)
        y_ref[:, j * ublk:(j + 1) * ublk] = jnp.dot(hs_ref[...].astype(BF16), wc_ref[j],
                                                    preferred_element_type=F32)
    o_ref[...] = _glu_out(y_ref[...], u_ref[...].astype(F32), d_ref, wg_ref, bg_ref).astype(o_ref.dtype)
    hf_ref[0] = carry_ref[...]


def _s5_prompt(u_perm, sw, d_skip, w_glu, b_glu, n_batch, chunk):
    rows, width = u_perm.shape
    ar, ai, apr, api, wb, wc = sw
    nblk, ublk, two_cw = wb.shape
    n_state = ar.shape[1]
    nchunk = rows // n_batch // chunk
    const2 = lambda b, c: (0, 0)
    const3 = lambda b, c: (0, 0, 0)
    kern = functools.partial(_s5_prompt_kernel, n_steps=chunk // SUBSEQ)
    return pl.pallas_call(
        kern,
        grid=(n_batch, nchunk),
        in_specs=[
            pl.BlockSpec((chunk, width), lambda b, c: (b * nchunk + c, 0)),
            pl.BlockSpec((1, n_state), const2), pl.BlockSpec((1, n_state), const2),
            pl.BlockSpec((1, n_state), const2), pl.BlockSpec((1, n_state), const2),
            pl.BlockSpec(wb.shape, const3), pl.BlockSpec(wc.shape, const3),
            pl.BlockSpec((1, width), const2), pl.BlockSpec((width, width), const2),
            pl.BlockSpec((1, width), const2),
        ],
        out_specs=[pl.BlockSpec((chunk, width), lambda b, c: (b * nchunk + c, 0)),
                   pl.BlockSpec((1, 1, 2 * n_state), lambda b, c: (b, 0, 0))],
        out_shape=[jax.ShapeDtypeStruct((rows, width), BF16),
                   jax.ShapeDtypeStruct((n_batch, 1, 2 * n_state), F32)],
        scratch_shapes=[pltpu.VMEM((chunk, two_cw), F32), pltpu.VMEM((chunk, two_cw), F32),
                        pltpu.VMEM((chunk, width), F32),
                        pltpu.VMEM((1, 2 * n_state), F32)],
        compiler_params=_params(("arbitrary", "arbitrary")),
        name="s5_prompt",
    )(u_perm, ar, ai, apr, api, wb, wc, d_skip.reshape(1, -1), w_glu.astype(BF16),
      b_glu.reshape(1, -1))


def _s5_sample_kernel(u_ref, h0r_ref, h0i_ref, ar_ref, ai_ref, wb_ref, wc_ref, d_ref, wg_ref,
                      bg_ref, o_ref, hr_ref, hi_ref, xs_ref, y_ref, *, t_new):
    nblk, ublk, two_cw = wb_ref.shape
    cw = two_cw // 2
    n_seq = h0r_ref.shape[0]
    for j in range(nblk):
        xs_ref[...] = jnp.dot(u_ref[:, j * ublk:(j + 1) * ublk], wb_ref[j],
                              preferred_element_type=F32)
        for lb in range(cw // LANE_WIN):
            re = slice(lb * LANE_WIN, (lb + 1) * LANE_WIN)
            im = slice(cw + lb * LANE_WIN, cw + (lb + 1) * LANE_WIN)
            gre = slice(j * cw + lb * LANE_WIN, j * cw + (lb + 1) * LANE_WIN)
            ar = jnp.broadcast_to(ar_ref[:, gre], (SUBLANES, LANE_WIN))
            ai = jnp.broadcast_to(ai_ref[:, gre], (SUBLANES, LANE_WIN))

            def group_body(rg, carry):
                base = pl.multiple_of(rg * SUBLANES, SUBLANES)
                h = (h0r_ref[pl.ds(base, SUBLANES), gre], h0i_ref[pl.ds(base, SUBLANES), gre])
                for t in range(t_new):
                    r = pl.ds(t * n_seq + base, SUBLANES)
                    h = _cstep(ar, ai, h[0], h[1], xs_ref[r, re], xs_ref[r, im])
                    xs_ref[r, re] = h[0]
                    xs_ref[r, im] = h[1]
                hr_ref[pl.ds(base, SUBLANES), gre] = h[0]
                hi_ref[pl.ds(base, SUBLANES), gre] = h[1]
                return carry

            lax.fori_loop(0, n_seq // SUBLANES, group_body, 0)
        y_ref[:, j * ublk:(j + 1) * ublk] = jnp.dot(xs_ref[...].astype(BF16), wc_ref[j],
                                                    preferred_element_type=F32)
    o_ref[...] = _glu_out(y_ref[...], u_ref[...].astype(F32), d_ref, wg_ref, bg_ref).astype(o_ref.dtype)


def _s5_sample(u_tm, h0r, h0i, sw, d_skip, w_glu, b_glu, t_new):
    rows, width = u_tm.shape
    ar, ai, _, _, wb, wc = sw
    two_cw = wb.shape[2]
    n_seq, n_state = h0r.shape
    kern = functools.partial(_s5_sample_kernel, t_new=t_new)
    return pl.pallas_call(
        kern,
        out_shape=[jax.ShapeDtypeStruct((rows, width), BF16),
                   jax.ShapeDtypeStruct((n_seq, n_state), F32),
                   jax.ShapeDtypeStruct((n_seq, n_state), F32)],
        scratch_shapes=[pltpu.VMEM((rows, two_cw), F32), pltpu.VMEM((rows, width), F32)],
        compiler_params=pltpu.CompilerParams(vmem_limit_bytes=VMEM_LIMIT),
        name="s5_sample",
    )(u_tm, h0r, h0i, ar, ai, wb, wc, d_skip.reshape(1, -1), w_glu.astype(BF16),
      b_glu.reshape(1, -1))


def _ffn_kernel(te_ref, nu_ref, x_ref, g_ref, wg_ref, wu_ref, wd_ref, o_ref, hn_ref, acc_ref,
                *, residual):
    i, j = pl.program_id(0), pl.program_id(1)

    @pl.when(i < nu_ref[0])
    def _():
        @pl.when(j == 0)
        def _():
            hn_ref[...] = _rms(x_ref[...], g_ref[...]).astype(BF16)
            acc_ref[...] = jnp.zeros_like(acc_ref)

        h = hn_ref[...]
        gate = jnp.dot(h, wg_ref[0].astype(BF16), preferred_element_type=F32)
        up = jnp.dot(h, wu_ref[0].astype(BF16), preferred_element_type=F32)
        act = (gate * jax.nn.sigmoid(gate) * up).astype(BF16)
        acc_ref[...] += jnp.dot(act, wd_ref[0].astype(BF16), preferred_element_type=F32)

        @pl.when(j == pl.num_programs(1) - 1)
        def _():
            o_ref[...] = (x_ref[...] + acc_ref[...]) if residual else acc_ref[...]

    @pl.when((i >= nu_ref[0]) & (j == 0))
    def _():
        o_ref[...] = jnp.zeros_like(o_ref)


def _ffn(x, g, w_gu, w_down, tile_expert, n_used, tm, tf, residual):
    rows, d = x.shape
    d_ff = w_down.shape[1]
    nf = d_ff // tf
    row_map = lambda i, j, te, nu: (jnp.maximum(jnp.minimum(i, nu[0] - 1), 0), 0)
    jj = lambda i, j, nu: jnp.where(i < nu[0], j, nf - 1)
    kern = functools.partial(_ffn_kernel, residual=residual)
    grid_spec = pltpu.PrefetchScalarGridSpec(
        num_scalar_prefetch=2,
        grid=(rows // tm, nf),
        in_specs=[
            pl.BlockSpec((tm, d), row_map),
            pl.BlockSpec((1, d), lambda i, j, te, nu: (0, 0)),
            pl.BlockSpec((1, d, tf), lambda i, j, te, nu: (te[i], 0, jj(i, j, nu))),
            pl.BlockSpec((1, d, tf), lambda i, j, te, nu: (te[i], 0, nf + jj(i, j, nu))),
            pl.BlockSpec((1, tf, d), lambda i, j, te, nu: (te[i], jj(i, j, nu), 0)),
        ],
        out_specs=pl.BlockSpec((tm, d), lambda i, j, te, nu: (i, 0)),
        scratch_shapes=[pltpu.VMEM((tm, d), BF16), pltpu.VMEM((tm, d), F32)],
    )
    return pl.pallas_call(
        kern,
        grid_spec=grid_spec,
        out_shape=jax.ShapeDtypeStruct((rows, d), F32),
        compiler_params=_params(("arbitrary", "arbitrary")),
        name="swiglu_ffn",
    )(tile_expert, n_used, x, g.reshape(1, d), w_gu, w_gu, w_down)


def _two_source(i, na, a_ref, b_ref):
    return jnp.where(i < na, a_ref[...], b_ref[...])


def _router_kernel(xa_ref, xb_ref, g_ref, wr_ref, eid_ref, rank_ref, gate_ref, cnt_ref,
                   carry_ref, *, na):
    i = pl.program_id(0)
    tr = xa_ref.shape[0]
    n_exp = wr_ref.shape[1]

    @pl.when(i == 0)
    def _():
        carry_ref[...] = jnp.zeros_like(carry_ref)

    h = _rms(_two_source(i, na, xa_ref, xb_ref), g_ref[...])
    w = wr_ref[...]
    h_hi, w_hi = h.astype(BF16), w.astype(BF16)
    h_lo = (h - h_hi.astype(F32)).astype(BF16)
    w_lo = (w - w_hi.astype(F32)).astype(BF16)
    logits = (jnp.dot(h_hi, w_hi, preferred_element_type=F32)
              + (jnp.dot(h_hi, w_lo, preferred_element_type=F32)
                 + jnp.dot(h_lo, w_hi, preferred_element_type=F32)))
    idx = lax.broadcasted_iota(I32, (tr, n_exp), 1).astype(F32)
    m1 = jnp.max(logits, axis=-1, keepdims=True)
    i1 = jnp.min(jnp.where(logits == m1, idx, float(n_exp)), axis=-1, keepdims=True)
    rest = jnp.where(idx == i1, -jnp.inf, logits)
    m2 = jnp.max(rest, axis=-1, keepdims=True)
    i2 = jnp.min(jnp.where(rest == m2, idx, float(n_exp)), axis=-1, keepdims=True)
    e2 = jnp.exp(m2 - m1)
    den = 1.0 + e2
    oh1 = (idx == i1).astype(F32)
    oh2 = (idx == i2).astype(F32)
    both = oh1 + oh2
    r = lax.broadcasted_iota(I32, (tr, tr), 0)
    c = lax.broadcasted_iota(I32, (tr, tr), 1)
    tri = jnp.where(c < r, 1.0, 0.0).astype(BF16)
    before = jnp.dot(tri, both.astype(BF16), preferred_element_type=F32) + carry_ref[...]
    rank1 = jnp.sum(oh1 * before, axis=-1, keepdims=True)
    rank2 = jnp.sum(oh2 * before, axis=-1, keepdims=True)
    carry_ref[...] += jnp.sum(both, axis=0, keepdims=True)
    k = lax.broadcasted_iota(I32, (tr, 2), 1)
    eid_ref[...] = jnp.where(k == 0, i1, i2).astype(I32)
    rank_ref[...] = jnp.where(k == 0, rank1, rank2).astype(I32)
    gate_ref[...] = jnp.where(k == 0, 1.0 / den, e2 / den)
    cnt_ref[...] = carry_ref[...].astype(I32)


def _router(xa, xb, g, w_router, tr):
    d = xa.shape[1]
    na, nb = xa.shape[0] // tr, xb.shape[0] // tr
    n_exp = w_router.shape[1]
    rows = xa.shape[0] + xb.shape[0]
    amap = lambda i: (jnp.minimum(i, na - 1), 0)
    bmap = lambda i: (jnp.maximum(i - na, 0), 0)
    pair = lambda dt: jax.ShapeDtypeStruct((rows, 2), dt)
    return pl.pallas_call(
        functools.partial(_router_kernel, na=na),
        grid=(na + nb,),
        in_specs=[pl.BlockSpec((tr, d), amap), pl.BlockSpec((tr, d), bmap),
                  pl.BlockSpec((1, d), lambda i: (0, 0)),
                  pl.BlockSpec((d, n_exp), lambda i: (0, 0))],
        out_specs=[pl.BlockSpec((tr, 2), lambda i: (i, 0))] * 3
                  + [pl.BlockSpec((1, n_exp), lambda i: (0, 0))],
        out_shape=[pair(I32), pair(I32), pair(F32), jax.ShapeDtypeStruct((1, n_exp), I32)],
        scratch_shapes=[pltpu.VMEM((1, n_exp), F32)],
        compiler_params=_params(("arbitrary",)),
        name="moe_router",
    )(xa, xb, g.reshape(1, d), w_router)


DMA_UNROLL = 8


def _dispatch_kernel(fill_ref, pos_ref, xa_ref, xb_ref, o_ref, zero_ref, sem, zsem, *, na, td, n_exp):
    i = pl.program_id(0)
    tm = zero_ref.shape[0]

    def run(x_ref):
        def copy(row, dst):
            return pltpu.make_async_copy(x_ref.at[pl.ds(row, 1)], o_ref.at[pl.ds(dst, 1)], sem)

        def issue(g, carry):
            for u in range(DMA_UNROLL):
                r = g * DMA_UNROLL + u
                copy(r, pos_ref[0, 0, 2 * r]).start(priority=0)
                copy(r, pos_ref[0, 0, 2 * r + 1]).start(priority=1)
            return carry
        lax.fori_loop(0, td // DMA_UNROLL, issue, 0)

        def drain(g, carry):
            for _ in range(2 * DMA_UNROLL):
                copy(0, 0).wait()
            return carry
        lax.fori_loop(0, td // DMA_UNROLL, drain, 0)

        @pl.when(i == pl.num_programs(0) - 1)
        def _():
            for e in range(n_exp):
                lo, hi = fill_ref[2 * e], fill_ref[2 * e + 1]

                def fill(r, carry):
                    copy(0, r).start()
                    return carry
                lax.fori_loop(lo, hi, fill, 0)

                def drain_one(r, carry):
                    copy(0, 0).wait()
                    return carry
                lax.fori_loop(lo, hi, drain_one, 0)

    @pl.when(i < na)
    def _():
        run(xa_ref)

    @pl.when(i >= na)
    def _():
        run(xb_ref)

    @pl.when(i == pl.num_programs(0) - 1)
    def _():
        zero_ref[...] = jnp.zeros_like(zero_ref)

        def zero_tile(t, carry):
            cp = pltpu.make_async_copy(zero_ref, o_ref.at[pl.ds(pl.multiple_of(t * tm, tm), tm)], zsem)
            cp.start()
            cp.wait()
            return carry
        lax.fori_loop(fill_ref[2 * n_exp], o_ref.shape[0] // tm, zero_tile, 0)


def _dispatch(xa, xb, pos, fill, rows_out, td, tm, n_exp):
    d = xa.shape[1]
    na, nb = xa.shape[0] // td, xb.shape[0] // td
    grid_spec = pltpu.PrefetchScalarGridSpec(
        num_scalar_prefetch=1,
        grid=(na + nb,),
        in_specs=[pl.BlockSpec((1, 1, 2 * td), lambda i, f: (i, 0, 0), memory_space=pltpu.SMEM),
                  pl.BlockSpec((td, d), lambda i, f: (jnp.minimum(i, na - 1), 0)),
                  pl.BlockSpec((td, d), lambda i, f: (jnp.maximum(i - na, 0), 0))],
        out_specs=pl.BlockSpec(memory_space=pl.ANY),
        scratch_shapes=[pltpu.VMEM((tm, d), F32), pltpu.SemaphoreType.DMA, pltpu.SemaphoreType.DMA],
    )
    return pl.pallas_call(
        functools.partial(_dispatch_kernel, na=na, td=td, n_exp=n_exp),
        grid_spec=grid_spec,
        out_shape=jax.ShapeDtypeStruct((rows_out, d), F32),
        compiler_params=_params(("arbitrary",)),
        name="moe_dispatch",
    )(fill, pos.reshape(na + nb, 1, 2 * td), xa, xb)


def _combine_kernel(pos_ref, posn_ref, xa_ref, xb_ref, gate_ref, g_ref, y_ref, oa_ref, ob_ref,
                    buf_ref, sem, *, na):
    i = pl.program_id(0)
    tc = xa_ref.shape[0]
    slot = lax.rem(i, 2)

    def copy(s, k, r, src):
        return pltpu.make_async_copy(y_ref.at[pl.ds(src, 1)], buf_ref.at[s, k, pl.ds(r, 1)],
                                     sem.at[s])

    def issue(p_ref, s):
        def body(g, carry):
            for u in range(DMA_UNROLL):
                r = g * DMA_UNROLL + u
                copy(s, 0, r, p_ref[0, 0, 2 * r]).start(priority=0)
                copy(s, 1, r, p_ref[0, 0, 2 * r + 1]).start(priority=1)
            return carry
        lax.fori_loop(0, tc // DMA_UNROLL, body, 0)

    @pl.when(i == 0)
    def _():
        issue(pos_ref, 0)

    @pl.when(i + 1 < pl.num_programs(0))
    def _():
        issue(posn_ref, 1 - slot)

    def drain(g, carry):
        for _ in range(2 * DMA_UNROLL):
            copy(slot, 0, 0, 0).wait()
        return carry
    lax.fori_loop(0, tc // DMA_UNROLL, drain, 0)

    x = _two_source(i, na, xa_ref, xb_ref)
    gate = gate_ref[...]
    out = _rms(x + (gate[:, 0:1] * buf_ref[slot, 0] + gate[:, 1:2] * buf_ref[slot, 1]), g_ref[...])

    @pl.when(i < na)
    def _():
        oa_ref[...] = out

    @pl.when(i >= na)
    def _():
        ob_ref[...] = out


def _combine(xa, xb, gates, pos, y_sorted, g_final, tc):
    d = xa.shape[1]
    na, nb = xa.shape[0] // tc, xb.shape[0] // tc
    n = na + nb
    amap = lambda i: (jnp.minimum(i, na - 1), 0)
    bmap = lambda i: (jnp.maximum(i - na, 0), 0)
    pos = pos.reshape(n, 1, 2 * tc)
    return pl.pallas_call(
        functools.partial(_combine_kernel, na=na),
        grid=(n,),
        in_specs=[pl.BlockSpec((1, 1, 2 * tc), lambda i: (i, 0, 0), memory_space=pltpu.SMEM),
                  pl.BlockSpec((1, 1, 2 * tc), lambda i: (jnp.minimum(i + 1, n - 1), 0, 0),
                               memory_space=pltpu.SMEM),
                  pl.BlockSpec((tc, d), amap), pl.BlockSpec((tc, d), bmap),
                  pl.BlockSpec((tc, 2), lambda i: (i, 0)),
                  pl.BlockSpec((1, d), lambda i: (0, 0)),
                  pl.BlockSpec(memory_space=pl.ANY)],
        out_specs=[pl.BlockSpec((tc, d), amap), pl.BlockSpec((tc, d), bmap)],
        out_shape=[jax.ShapeDtypeStruct(xa.shape, F32), jax.ShapeDtypeStruct(xb.shape, F32)],
        scratch_shapes=[pltpu.VMEM((2, 2, tc, d), F32), pltpu.SemaphoreType.DMA((2,))],
        compiler_params=_params(("arbitrary",)),
        name="moe_combine",
    )(pos, pos, xa, xb, gates, g_final.reshape(1, d), y_sorted)


def _moe(xa, xb, g_ffn, w_router, w_gu, w_down, g_final, tm, tf):
    n_exp = w_router.shape[1]
    rows = xa.shape[0] + xb.shape[0]
    eid, rank, gates, counts = _router(xa, xb, g_ffn, w_router, tr=512)
    counts = counts[0]
    padded = (counts + tm - 1) // tm * tm
    ends = jnp.cumsum(padded)
    offs = ends - padded
    pos = (offs[eid] + rank).reshape(-1)
    n_tiles = (2 * rows + n_exp * (tm - 1)) // tm
    n_used = (ends[-1] // tm).astype(I32)
    tile_start = jnp.arange(n_tiles, dtype=I32) * tm
    tile_expert = jnp.sum(tile_start[:, None] >= ends[None, :], axis=1).astype(I32)
    last = jnp.take(tile_expert, jnp.maximum(n_used - 1, 0))
    tile_expert = jnp.minimum(jnp.where(tile_start < ends[-1], tile_expert, last), n_exp - 1)
    fill = jnp.concatenate([jnp.stack([offs + counts, offs + padded], axis=1).reshape(-1),
                            n_used.reshape(1)]).astype(I32)
    x_sorted = _dispatch(xa, xb, pos, fill, n_tiles * tm, td=512, tm=tm, n_exp=n_exp)
    y_sorted = _ffn(x_sorted, g_ffn, w_gu, w_down, tile_expert, n_used.reshape(1), tm, tf,
                    residual=False)
    return _combine(xa, xb, gates, pos, y_sorted, g_final, tc=256)


def kernel(x_prompt, x_sample, mem_prompt, cache_swa_k, cache_swa_v, state_ssm_re, state_ssm_im, cache_mem_k, cache_mem_v, g_mix, g_ffn, g_mem, g_final, w_mem_kv, w_in_a, sinks, w_out_a, w_in_b, lam_re, lam_im, log_step, b_re, b_im, c_re, c_im, d_skip, w_glu, b_glu, w_out_b, w_ffn_gu, w_ffn_down, w_router, w_exp_gu, w_exp_down):
    n_batch, seq, d = x_prompt.shape
    n_dec, t_new, _ = x_sample.shape
    n_mem = mem_prompt.shape[1]
    depth = g_mix.shape[0]
    assert depth == 2 and w_in_a.shape[0] == 1 and w_in_b.shape[0] == 1
    n_kv = cache_swa_k.shape[3]
    n_q = sinks.shape[1]
    group = n_q // n_kv
    n_x = cache_mem_k.shape[3]
    x_q = n_x * HEAD_DIM
    kv_a = n_kv * HEAD_DIM
    q_a = n_q * HEAD_DIM
    s5_width = w_glu.shape[1]
    n_groups, n_state_g = lam_re.shape[1:]
    n_state = n_groups * n_state_g
    cache_w = cache_swa_k.shape[2]

    xp = x_prompt.reshape(n_batch * seq, d)
    xs = x_sample.reshape(n_dec * t_new, d)
    mem = mem_prompt.reshape(n_batch * n_mem, d)
    bt = 16
    chunk = 1024
    sub_len = chunk // SUBSEQ

    memkv = [_proj(mem, g_mem[i], w_mem_kv[i], tm=n_mem) for i in range(depth)]
    mem_k_prompt = jnp.stack([m[:, :x_q].reshape(n_batch, n_mem, n_x, HEAD_DIM) for m in memkv])
    mem_v_prompt = jnp.stack([m[:, x_q:].reshape(n_batch, n_mem, n_x, HEAD_DIM) for m in memkv])

    w_in = jnp.concatenate(
        [w_in_a[0][:, :q_a].reshape(d, n_kv, group, HEAD_DIM).transpose(0, 2, 1, 3).reshape(d, q_a),
         w_in_a[0][:, q_a:]], axis=1)
    w_out = jnp.concatenate(
        [w_out_a[0][:q_a].reshape(n_kv, group, HEAD_DIM, d).transpose(1, 0, 2, 3).reshape(q_a, d),
         w_out_a[0][q_a:]], axis=0)
    sink = sinks[0].reshape(n_kv, group).T.reshape(-1)
    pa_p = _proj(xp, g_mix[0], w_in, tm=512)
    pa_s = _proj(xs, g_mix[0], w_in, tm=512)
    x1p = _mixa_prompt(pa_p, xp, memkv[0], sink, w_out, n_batch, n_kv, group, n_x, n_mem)
    x1s, swa_ks, swa_vs = _mixa_sample(
        pa_s, xs, cache_swa_k[0].reshape(n_dec, cache_w, kv_a), cache_swa_v[0].reshape(n_dec, cache_w, kv_a),
        cache_mem_k[0].reshape(n_dec, n_mem, x_q), cache_mem_v[0].reshape(n_dec, n_mem, x_q),
        sink, w_out, n_kv, group, n_x, t_new, bt)
    w_keep = min(WINDOW, seq)
    kv_p = pa_p.reshape(n_batch, seq, -1)[:, seq - w_keep:, q_a:q_a + 2 * kv_a]
    swa_k_prompt = kv_p[..., :kv_a].reshape(1, n_batch, w_keep, n_kv, HEAD_DIM)
    swa_v_prompt = kv_p[..., kv_a:].reshape(1, n_batch, w_keep, n_kv, HEAD_DIM)
    swa_k_sample = swa_ks.reshape(1, n_dec, cache_w, n_kv, HEAD_DIM)
    swa_v_sample = swa_vs.reshape(1, n_dec, cache_w, n_kv, HEAD_DIM)

    x2p = _ffn(x1p, g_ffn[0], w_ffn_gu, w_ffn_down, jnp.zeros((x1p.shape[0] // 1024,), I32),
               jnp.full((1,), x1p.shape[0] // 1024, I32), 1024, 512, residual=True)
    x2s = _ffn(x1s, g_ffn[0], w_ffn_gu, w_ffn_down, jnp.zeros((x1s.shape[0] // 1024,), I32),
               jnp.full((1,), x1s.shape[0] // 1024, I32), 1024, 512, residual=True)

    pb_p = _proj(x2p, g_mix[1], w_in_b[0], tm=512)
    pb_s = _proj(x2s, g_mix[1], w_in_b[0], tm=512)
    sw = _s5_weights(lam_re[0], lam_im[0], log_step[0], b_re[0], b_im[0], c_re[0], c_im[0], sub_len)
    nchunk = seq // chunk
    u_perm = (pb_p[:, :s5_width].astype(BF16)
              .reshape(n_batch, nchunk, SUBSEQ, sub_len, s5_width)
              .transpose(0, 1, 3, 2, 4).reshape(n_batch * seq, s5_width))
    so_perm, h_fin = _s5_prompt(u_perm, sw, d_skip[0], w_glu[0], b_glu[0], n_batch, chunk)
    so_p = (so_perm.reshape(n_batch, nchunk, sub_len, SUBSEQ, s5_width)
            .transpose(0, 1, 3, 2, 4).reshape(n_batch * seq, s5_width))
    ssm_re_prompt = h_fin[:, 0, :n_state].reshape(1, n_batch, n_groups, n_state_g)
    ssm_im_prompt = h_fin[:, 0, n_state:].reshape(1, n_batch, n_groups, n_state_g)

    u_tm = (pb_s[:, :s5_width].astype(BF16).reshape(n_dec, t_new, s5_width)
            .transpose(1, 0, 2).reshape(n_dec * t_new, s5_width))
    so_tm, hr_s, hi_s = _s5_sample(u_tm, state_ssm_re[0].reshape(n_dec, n_state),
                                   state_ssm_im[0].reshape(n_dec, n_state), sw,
                                   d_skip[0], w_glu[0], b_glu[0], t_new)
    so_s = so_tm.reshape(t_new, n_dec, s5_width).transpose(1, 0, 2).reshape(n_dec * t_new, s5_width)
    ssm_re_sample = hr_s.reshape(1, n_dec, n_groups, n_state_g)
    ssm_im_sample = hi_s.reshape(1, n_dec, n_groups, n_state_g)

    x3p = _tailb_prompt(so_p, pb_p, x2p, memkv[1], w_out_b[0], n_batch, n_x, n_mem, tm=512)
    x3s = _tailb_sample(so_s, pb_s, x2s, cache_mem_k[1].reshape(n_dec, n_mem, x_q),
                        cache_mem_v[1].reshape(n_dec, n_mem, x_q), w_out_b[0], n_x, t_new, bt)

    yp, ys = _moe(x3p, x3s, g_ffn[1], w_router[0], w_exp_gu[0], w_exp_down[0], g_final, 1024, 512)
    return (yp.reshape(n_batch, seq, d), ys.reshape(n_dec, t_new, d),
            swa_k_prompt, swa_v_prompt, ssm_re_prompt, ssm_im_prompt,
            mem_k_prompt, mem_v_prompt, swa_k_sample, swa_v_sample,
            ssm_re_sample, ssm_im_sample)
```
